```python
import jax
import jax.numpy as jnp
from jax import lax
import numpy as np

D_MODEL = 1024
BATCH = 1
SEQ = 16384
DEPTH = 4

GRID_W = 64
CTX_LEN = 256
N_MIXERS = 3
N_CONV_LAYERS = (DEPTH + 2) // N_MIXERS
N_RET_LAYERS = (DEPTH + 1) // N_MIXERS
N_ATT_LAYERS = DEPTH // N_MIXERS
CONV_WIDTH = 31
RET_HEADS = 4
RET_QK_DIM = D_MODEL // RET_HEADS
RET_V_DIM = 2 * D_MODEL // RET_HEADS
RET_CHUNK = 128
ATT_Q_HEADS = 16
ATT_KV_HEADS = 4
ATT_HEAD_DIM = 64
ATT_WINDOW = 128
ATT_BLOCK = 128
FFN_HIDDEN = -(-8 * D_MODEL // (3 * 256)) * 256
ROPE_BASE = 10000.0
NORM_EPS = 1e-6
NEG_INF = -1e30

kernel_name = 'hybrid_conv_retention_swa_dit'


def rms_norm(x, g):
    xf = x.astype(jnp.float32)
    y = xf * lax.rsqrt(jnp.mean(xf * xf, axis=-1, keepdims=True) + NORM_EPS)
    return (y * g.astype(jnp.float32)).astype(x.dtype)


def layer_norm(x, g):
    xf = x.astype(jnp.float32)
    xc = xf - jnp.mean(xf, axis=-1, keepdims=True)
    y = xc * lax.rsqrt(jnp.mean(xc * xc, axis=-1, keepdims=True) + NORM_EPS)
    return (y * g.astype(jnp.float32)).astype(x.dtype)


def modulate(x, g, shift, scale):
    return rms_norm(x, g) * (1 + scale) + shift


def adaln(cond, w, b):
    mods = jnp.split(jax.nn.silu(cond) @ w + b, 6, axis=-1)
    return [m[:, None, :] for m in mods]


def grid_positions(T):
    n_rows = T // GRID_W
    rows = jnp.repeat(jnp.arange(n_rows), GRID_W).astype(jnp.float32)
    cols = (jnp.arange(n_rows * GRID_W) % GRID_W).astype(jnp.float32)
    return rows, cols


def _rotate(x, ang):
    m = x.shape[-1] // 2
    cos = jnp.cos(ang)[None, :, None, :]
    sin = jnp.sin(ang)[None, :, None, :]
    x1, x2 = x[..., :m], x[..., m:]
    return jnp.concatenate([x1 * cos - x2 * sin, x2 * cos + x1 * sin], axis=-1)


def axial_rope(x, rows, cols):
    hd = x.shape[-1]
    half, quarter = hd // 2, hd // 4
    inv = ROPE_BASE ** (-jnp.arange(quarter, dtype=jnp.float32) / quarter)
    xf = x.astype(jnp.float32)
    out = jnp.concatenate([_rotate(xf[..., :half], rows[:, None] * inv),
                           _rotate(xf[..., half:], cols[:, None] * inv)], axis=-1)
    return out.astype(x.dtype)


def conv_module(h, w1, b1, dw, dw_b, norm_g, w2):
    y = h @ w1 + b1
    a, gt = jnp.split(y, 2, axis=-1)
    y = a * jax.nn.sigmoid(gt)
    y = lax.conv_general_dilated(
        y, dw[:, None, :].astype(y.dtype), window_strides=(1,),
        padding=[(CONV_WIDTH // 2, CONV_WIDTH // 2)],
        dimension_numbers=('NWC', 'WIO', 'NWC'),
        feature_group_count=D_MODEL) + dw_b
    y = jax.nn.silu(layer_norm(y, norm_g))
    return y @ w2


def retention_chunked(q, k, v, log_gamma, s0):
    B, H, T, dk = q.shape
    dv = v.shape[-1]
    C = RET_CHUNK
    N = T // C
    idx = jnp.arange(C, dtype=jnp.float32)
    rel = idx[:, None] - idx[None, :]
    intra_decay = jnp.where(rel >= 0, jnp.exp(log_gamma[:, None, None] * jnp.maximum(rel, 0.0)), 0.0)
    q_decay = jnp.exp(log_gamma[:, None] * (idx + 1.0))[None, :, :, None]
    k_decay = jnp.exp(log_gamma[:, None] * (C - 1.0 - idx))[None, :, :, None]
    chunk_decay = jnp.exp(log_gamma * C)[None, :, None, None]
    lead = lambda t: jnp.moveaxis(t.reshape(B, H, N, C, t.shape[-1]), 2, 0)

    def step(s, xs):
        qc, kc, vc = xs
        scores = jnp.einsum('bhcd,bhmd->bhcm', qc, kc) * intra_decay
        o = jnp.einsum('bhcm,bhme->bhce', scores, vc) + jnp.einsum('bhcd,bhde->bhce', qc * q_decay, s)
        s = s * chunk_decay + jnp.einsum('bhcd,bhce->bhde', kc * k_decay, vc)
        return s, o

    s_final, o = lax.scan(step, s0, (lead(q), lead(k), lead(v)))
    return jnp.moveaxis(o, 0, 2).reshape(B, H, T, dv), s_final


def retention_state(k, v, log_gamma, reverse):
    L = k.shape[2]
    m = jnp.arange(L, dtype=jnp.float32)
    dist = m if reverse else (L - 1.0 - m)
    w = jnp.exp(log_gamma[:, None] * dist)[None, :, :, None]
    return jnp.einsum('bhld,bhle->bhde', k * w, v)


def retention_mixer(hc, hx, w_in, dec_f, dec_b, w_out, rows, cols, ctx_out):
    H, dk, dv = RET_HEADS, RET_QK_DIM, RET_V_DIM

    def project(h, rope):
        B, T, _ = h.shape
        q, k, v, g = jnp.split(h @ w_in, [H * dk, 2 * H * dk, 2 * H * dk + H * dv], axis=-1)
        q = q.reshape(B, T, H, dk)
        k = k.reshape(B, T, H, dk)
        if rope:
            q = axial_rope(q, rows, cols)
            k = axial_rope(k, rows, cols)
        heads = lambda t: jnp.swapaxes(t.astype(jnp.float32), 1, 2)
        return heads(q), heads(k) * dk ** -0.5, heads(v.reshape(B, T, H, dv)), g

    def readout(o, g):
        B, _, T, _ = o.shape
        o = o * lax.rsqrt(jnp.mean(o * o, axis=-1, keepdims=True) + NORM_EPS)
        o = jnp.swapaxes(o, 1, 2).reshape(B, T, H * dv).astype(g.dtype)
        return (jax.nn.silu(g) * o) @ w_out

    flip = lambda t: jnp.flip(t, axis=2)
    lg_f = jax.nn.log_sigmoid(dec_f.astype(jnp.float32))
    lg_b = jax.nn.log_sigmoid(dec_b.astype(jnp.float32))
    qc, kc, vc, gc = project(hc, False)
    qx, kx, vx, gx = project(hx, True)
    if ctx_out:
        zeros = jnp.zeros((hc.shape[0], H, dk, dv), jnp.float32)
        oc_f, sc_f = retention_chunked(qc, kc, vc, lg_f, zeros)
        oc_b, sc_b = retention_chunked(flip(qc), flip(kc), flip(vc), lg_b, zeros)
        oc = readout(oc_f + flip(oc_b), gc)
    else:
        sc_f = retention_state(kc, vc, lg_f, False)
        sc_b = retention_state(kc, vc, lg_b, True)
        oc = None
    ox_f, _ = retention_chunked(qx, kx, vx, lg_f, sc_f)
    ox_b, _ = retention_chunked(flip(qx), flip(kx), flip(vx), lg_b, sc_b)
    ox = readout(ox_f + flip(ox_b), gx)
    return oc, ox


def attention_mixer(hc, hx, w_qkv, q_gain, k_gain, sink, w_o, rows, cols, ctx_out):
    Hq, Hk, hd = ATT_Q_HEADS, ATT_KV_HEADS, ATT_HEAD_DIM
    G = Hq // Hk

    def project(h, rope):
        B, T, _ = h.shape
        q, k, v = jnp.split(h @ w_qkv, [Hq * hd, (Hq + Hk) * hd], axis=-1)
        q = rms_norm(q.reshape(B, T, Hq, hd), q_gain)
        k = rms_norm(k.reshape(B, T, Hk, hd), k_gain)
        if rope:
            q = axial_rope(q, rows, cols)
            k = axial_rope(k, rows, cols)
        q = (q * hd ** -0.5).reshape(B, T, Hk, G, hd)
        return q, k, v.reshape(B, T, Hk, hd)

    sink_kg = sink.astype(jnp.float32).reshape(Hk, G)[None, :, :, None, None]

    def sink_softmax(parts):
        s_sink = jnp.broadcast_to(sink_kg, parts[0].shape[:-1] + (1,))
        return jax.nn.softmax(jnp.concatenate(parts + [s_sink], axis=-1), axis=-1)

    qc, kc, vc = project(hc, False)
    qx, kx, vx = project(hx, True)
    B, L = hc.shape[0], hc.shape[1]
    T = hx.shape[1]
    C = ATT_BLOCK
    N = T // C

    if ctx_out:
        s = jnp.einsum('bqkgd,bjkd->bkgqj', qc, kc).astype(jnp.float32)
        p = sink_softmax([s])[..., :L].astype(vc.dtype)
        oc = jnp.einsum('bkgqj,bjkd->bqkgd', p, vc).reshape(B, L, Hq * hd) @ w_o
    else:
        oc = None

    pad = lambda t: jnp.pad(t, ((0, 0), (C, C), (0, 0), (0, 0))).reshape(B, N + 2, C, Hk, hd)
    band = lambda tp: jnp.concatenate([tp[:, :-2], tp[:, 1:-1], tp[:, 2:]], axis=2)
    kb = band(pad(kx))
    vb = band(pad(vx))
    qb = qx.reshape(B, N, C, Hk, G, hd)
    a = jnp.arange(C)[:, None]
    j = jnp.arange(3 * C)[None, :]
    key_pos = jnp.arange(N)[:, None, None] * C + (j - C)[None]
    mask = (jnp.abs(j - C - a) <= ATT_WINDOW)[None] & (key_pos >= 0) & (key_pos < T)

    def one_block(args):
        q_n, k_n, v_n, m_n = args
        s_ctx = jnp.einsum('bqkgd,bjkd->bkgqj', q_n, kc).astype(jnp.float32)
        s_band = jnp.where(m_n, jnp.einsum('bqkgd,bjkd->bkgqj', q_n, k_n).astype(jnp.float32), NEG_INF)
        p = sink_softmax([s_ctx, s_band])
        o = jnp.einsum('bkgqj,bjkd->bqkgd', p[..., :L].astype(vc.dtype), vc)
        return o + jnp.einsum('bkgqj,bjkd->bqkgd', p[..., L:L + 3 * C].astype(v_n.dtype), v_n)

    lead = lambda t: jnp.moveaxis(t, 1, 0)
    ob = lax.map(one_block, (lead(qb), lead(kb), lead(vb), mask))
    ox = jnp.moveaxis(ob, 0, 1).reshape(B, T, Hq * hd) @ w_o
    return oc, ox


def swiglu_ffn(h, w_gu, w_down):
    a, b = jnp.split(h @ w_gu, 2, axis=-1)
    return (jax.nn.silu(a) * b) @ w_down


def setup_inputs(seed: int = 0) -> dict:
    key = jax.random.key(seed)
    ks = iter(jax.random.split(key, 32))
    nrm = lambda shape, scale: jax.random.normal(next(ks), shape, jnp.float32) * scale
    D = D_MODEL
    ret_proj = 2 * RET_HEADS * RET_QK_DIM + 2 * RET_HEADS * RET_V_DIM
    att_proj = (ATT_Q_HEADS + 2 * ATT_KV_HEADS) * ATT_HEAD_DIM
    gam = 1.0 - np.exp(np.linspace(np.log(1.0 / 32), np.log(1.0 / 512), RET_HEADS))
    dec_logit = jnp.asarray(np.log(gam / (1.0 - gam)), jnp.float32)
    return {
        'x': nrm((BATCH, SEQ, D), 1.0),
        'c': nrm((BATCH, D), 1.0),
        'ctx': nrm((BATCH, CTX_LEN, D), 1.0),
        'c_ctx': nrm((D,), 1.0),
        'ada_w': nrm((DEPTH, D, 6 * D), 0.5 * D ** -0.5),
        'ada_b': nrm((DEPTH, 6 * D), 0.01),
        'norm_mix': 1.0 + nrm((DEPTH, D), 0.02),
        'norm_ffn': 1.0 + nrm((DEPTH, D), 0.02),
        'conv_w1': nrm((N_CONV_LAYERS, D, 2 * D), D ** -0.5),
        'conv_b1': nrm((N_CONV_LAYERS, 2 * D), 0.01),
        'conv_dw': nrm((N_CONV_LAYERS, CONV_WIDTH, D), CONV_WIDTH ** -0.5),
        'conv_dw_b': nrm((N_CONV_LAYERS, D), 0.01),
        'conv_norm': 1.0 + nrm((N_CONV_LAYERS, D), 0.02),
        'conv_w2': nrm((N_CONV_LAYERS, D, D), D ** -0.5),
        'ret_w_in': nrm((N_RET_LAYERS, D, ret_proj), D ** -0.5),
        'ret_decay_f': dec_logit[None] + nrm((N_RET_LAYERS, RET_HEADS), 0.1),
        'ret_decay_b': dec_logit[None] + nrm((N_RET_LAYERS, RET_HEADS), 0.1),
        'ret_w_out': nrm((N_RET_LAYERS, RET_HEADS * RET_V_DIM, D), (RET_HEADS * RET_V_DIM) ** -0.5),
        'att_w_qkv': nrm((N_ATT_LAYERS, D, att_proj), D ** -0.5),
        'att_q_norm': 1.0 + nrm((N_ATT_LAYERS, ATT_HEAD_DIM), 0.02),
        'att_k_norm': 1.0 + nrm((N_ATT_LAYERS, ATT_HEAD_DIM), 0.02),
        'att_sink': nrm((N_ATT_LAYERS, ATT_Q_HEADS), 0.5),
        'att_w_o': nrm((N_ATT_LAYERS, ATT_Q_HEADS * ATT_HEAD_DIM, D), (ATT_Q_HEADS * ATT_HEAD_DIM) ** -0.5),
        'ffn_w_gu': nrm((DEPTH, D, 2 * FFN_HIDDEN), D ** -0.5),
        'ffn_w_down': nrm((DEPTH, FFN_HIDDEN, D), FFN_HIDDEN ** -0.5),
    }


def reference(x, c, ctx, c_ctx, ada_w, ada_b, norm_mix, norm_ffn,
              conv_w1, conv_b1, conv_dw, conv_dw_b, conv_norm, conv_w2,
              ret_w_in, ret_decay_f, ret_decay_b, ret_w_out,
              att_w_qkv, att_q_norm, att_k_norm, att_sink, att_w_o,
              ffn_w_gu, ffn_w_down):
    rows, cols = grid_positions(x.shape[1])
    h_ctx = ctx
    cond_ctx = c_ctx[None, :]
    for i in range(DEPTH):
        kind, j, last = i % N_MIXERS, i // N_MIXERS, i == DEPTH - 1
        sh_m, sc_m, g_m, sh_f, sc_f, g_f = adaln(c, ada_w[i], ada_b[i])
        hx = modulate(x, norm_mix[i], sh_m, sc_m)
        if (not last) or kind != 0:
            csh_m, csc_m, cg_m, csh_f, csc_f, cg_f = adaln(cond_ctx, ada_w[i], ada_b[i])
            hc = modulate(h_ctx, norm_mix[i], csh_m, csc_m)
        if kind == 0:
            conv_p = (conv_w1[j], conv_b1[j], conv_dw[j], conv_dw_b[j], conv_norm[j], conv_w2[j])
            ox = conv_module(hx, *conv_p)
            oc = None if last else conv_module(hc, *conv_p)
        elif kind == 1:
            oc, ox = retention_mixer(hc, hx, ret_w_in[j], ret_decay_f[j], ret_decay_b[j], ret_w_out[j],
                                     rows, cols, not last)
        else:
            oc, ox = attention_mixer(hc, hx, att_w_qkv[j], att_q_norm[j], att_k_norm[j], att_sink[j],
                                     att_w_o[j], rows, cols, not last)
        x = x + g_m * ox
        x = x + g_f * swiglu_ffn(modulate(x, norm_ffn[i], sh_f, sc_f), ffn_w_gu[i], ffn_w_down[i])
        if not last:
            h_ctx = h_ctx + cg_m * oc
            h_ctx = h_ctx + cg_f * swiglu_ffn(modulate(h_ctx, norm_ffn[i], csh_f, csc_f), ffn_w_gu[i], ffn_w_down[i])
    return x
```

```python
import functools

import jax
import jax.numpy as jnp
from jax import lax
from jax.experimental import pallas as pl
from jax.experimental.pallas import tpu as pltpu

f32 = jnp.float32
bf16 = jnp.bfloat16

D_MODEL = 1024
DEPTH = 4
GRID_W = 64
N_MIXERS = 3
CONV_WIDTH = 31
CONV_HALO = 16
RET_HEADS = 4
RET_QK_DIM = 256
RET_V_DIM = 512
RET_CHUNK = 256
ATT_Q_HEADS = 16
ATT_KV_HEADS = 4
ATT_GROUP = ATT_Q_HEADS // ATT_KV_HEADS
ATT_HEAD_DIM = 64
ATT_WINDOW = 128
ATT_BLOCK = 128
FFN_HIDDEN = 2816
FFN_CHUNK = 1408
ROPE_BASE = 10000.0
NORM_EPS = 1e-6
NEG_INF = -1e30
LANES = 128
VMEM_LIMIT = 56 * 1024 * 1024

SHIFT_M, SCALE_M, GATE_M, SHIFT_F, SCALE_F, GATE_F = range(6)


def _params(*sem):
    return pltpu.CompilerParams(dimension_semantics=sem, vmem_limit_bytes=VMEM_LIMIT)


def _resident(shape):
    nd = len(shape)
    return pl.BlockSpec(shape, lambda *_: (0,) * nd, pipeline_mode=pl.Buffered(1))


def _mod_spec(layer, which):
    return pl.BlockSpec((None, 8, D_MODEL), lambda *_: (layer, 0, which))


def _modulated(x, gain, scale, shift):
    ms = jnp.mean(x * x, axis=-1, keepdims=True)
    return (x * lax.rsqrt(ms + NORM_EPS)) * (gain * (1.0 + scale)) + shift


def _silu(v):
    return v * jax.nn.sigmoid(v)


def _adaln_kernel(c_ref, w_ref, b_ref, o_ref):
    s = _silu(c_ref[...])
    o_ref[...] = jnp.dot(s, w_ref[...], precision=lax.Precision.HIGHEST,
                         preferred_element_type=f32) + b_ref[...]


def _adaln(cond8, ada_w, ada_b):
    tn = 1536
    n = 6 * D_MODEL
    return pl.pallas_call(
        _adaln_kernel,
        out_shape=jax.ShapeDtypeStruct((DEPTH, 8, n), f32),
        grid=(DEPTH, n // tn),
        in_specs=[
            pl.BlockSpec((8, D_MODEL), lambda l, j: (0, 0)),
            pl.BlockSpec((None, D_MODEL, tn), lambda l, j: (l, 0, j)),
            pl.BlockSpec((None, 1, tn), lambda l, j: (l, 0, j)),
        ],
        out_specs=pl.BlockSpec((None, 8, tn), lambda l, j: (l, 0, j)),
        compiler_params=_params("arbitrary", "arbitrary"),
        name="adaln",
    )(cond8, ada_w, ada_b.reshape(DEPTH, 1, n))


def _conv_in_kernel(row, x_ref, g_ref, sh_ref, sc_ref, w_ref, b_ref, o_ref):
    h = _modulated(x_ref[...], g_ref[...], sc_ref[row:row + 1, :], sh_ref[row:row + 1, :])
    y = jnp.dot(h.astype(bf16), w_ref[...], preferred_element_type=f32) + b_ref[...]
    o_ref[...] = y[:, :D_MODEL] * jax.nn.sigmoid(y[:, D_MODEL:])


def _conv_in(x, mods, layer, row, gain, w1, b1, tm):
    t = x.shape[0]
    return pl.pallas_call(
        functools.partial(_conv_in_kernel, row),
        out_shape=jax.ShapeDtypeStruct((t, D_MODEL), f32),
        grid=(t // tm,),
        in_specs=[
            pl.BlockSpec((tm, D_MODEL), lambda i: (i, 0)),
            _resident((1, D_MODEL)),
            _mod_spec(layer, SHIFT_M),
            _mod_spec(layer, SCALE_M),
            _resident((D_MODEL, 2 * D_MODEL)),
            _resident((1, 2 * D_MODEL)),
        ],
        out_specs=pl.BlockSpec((tm, D_MODEL), lambda i: (i, 0)),
        compiler_params=_params("arbitrary"),
        name="conv_in",
    )(x, gain, mods, mods, w1, b1)


def _conv_out_kernel(row, n_tiles, um_ref, up_ref, un_ref, dw_ref, dwb_ref, ng_ref, w2_ref,
                     x_ref, gate_ref, o_ref, win_ref, y_ref):
    i = pl.program_id(0)
    tm = um_ref.shape[0]
    rb = 8
    win_ref[0:CONV_HALO, :] = jnp.where(i > 0, up_ref[...], 0.0)
    win_ref[CONV_HALO:CONV_HALO + tm, :] = um_ref[...]
    win_ref[CONV_HALO + tm:, :] = jnp.where(i < n_tiles - 1, un_ref[...], 0.0)
    off = CONV_HALO - CONV_WIDTH // 2

    def body(r, carry):
        base = pl.multiple_of(r * rb, rb)
        win = win_ref[pl.ds(base, rb + 2 * CONV_HALO), :]
        acc = jnp.broadcast_to(dwb_ref[...], (rb, D_MODEL))
        for k in range(CONV_WIDTH):
            acc = acc + win[off + k:off + k + rb, :] * dw_ref[k:k + 1, :]
        mu = jnp.mean(acc, axis=-1, keepdims=True)
        xc = acc - mu
        var = jnp.mean(xc * xc, axis=-1, keepdims=True)
        yn = xc * lax.rsqrt(var + NORM_EPS) * ng_ref[...]
        y_ref[pl.ds(base, rb), :] = _silu(yn)
        return carry

    lax.fori_loop(0, tm // rb, body, 0)
    o = jnp.dot(y_ref[...].astype(bf16), w2_ref[...], preferred_element_type=f32)
    o_ref[...] = x_ref[...] + gate_ref[row:row + 1, :] * o


def _conv_out(u, x, mods, layer, row, dw, dw_b, norm_g, w2, tm):
    t = x.shape[0]
    n_tiles = t // tm
    hb = tm // CONV_HALO
    n_hb = t // CONV_HALO
    return pl.pallas_call(
        functools.partial(_conv_out_kernel, row, n_tiles),
        out_shape=jax.ShapeDtypeStruct((t, D_MODEL), f32),
        grid=(n_tiles,),
        in_specs=[
            pl.BlockSpec((tm, D_MODEL), lambda i: (i, 0)),
            pl.BlockSpec((CONV_HALO, D_MODEL), lambda i: (jnp.maximum(i * hb - 1, 0), 0)),
            pl.BlockSpec((CONV_HALO, D_MODEL), lambda i: (jnp.minimum((i + 1) * hb, n_hb - 1), 0)),
            _resident((CONV_WIDTH, D_MODEL)),
            _resident((1, D_MODEL)),
            _resident((1, D_MODEL)),
            _resident((D_MODEL, D_MODEL)),
            pl.BlockSpec((tm, D_MODEL), lambda i: (i, 0)),
            _mod_spec(layer, GATE_M),
        ],
        out_specs=pl.BlockSpec((tm, D_MODEL), lambda i: (i, 0)),
        scratch_shapes=[
            pltpu.VMEM((tm + 2 * CONV_HALO, D_MODEL), f32),
            pltpu.VMEM((tm, D_MODEL), f32),
        ],
        compiler_params=_params("arbitrary"),
        name="conv_out",
    )(u, u, u, dw, dw_b, norm_g, w2, x, mods)


def _ret_in_kernel(row, x_ref, g_ref, sh_ref, sc_ref, w_ref, cos_ref, sin_ref,
                   q_ref, k_ref, v_ref, sg_ref):
    h = _modulated(x_ref[...], g_ref[...], sc_ref[row:row + 1, :], sh_ref[row:row + 1, :]).astype(bf16)
    hk = RET_HEADS * RET_QK_DIM
    hv = RET_HEADS * RET_V_DIM

    def rope(y, scale):
        outs = []
        for b in range(hk // LANES):
            yb = y[:, b * LANES:(b + 1) * LANES]
            tb = (b % 2) * LANES
            rot = yb * cos_ref[:, tb:tb + LANES] + pltpu.roll(yb, LANES // 2, axis=1) * sin_ref[:, tb:tb + LANES]
            outs.append(rot * scale)
        return jnp.concatenate(outs, axis=1)

    q = jnp.dot(h, w_ref[:, 0:hk], preferred_element_type=f32)
    q_ref[...] = rope(q, 1.0).astype(bf16)
    k = jnp.dot(h, w_ref[:, hk:2 * hk], preferred_element_type=f32)
    k_ref[...] = rope(k, RET_QK_DIM ** -0.5).astype(bf16)
    v_ref[...] = jnp.dot(h, w_ref[:, 2 * hk:2 * hk + hv], preferred_element_type=f32).astype(bf16)
    g = jnp.dot(h, w_ref[:, 2 * hk + hv:], preferred_element_type=f32)
    sg_ref[...] = _silu(g).astype(bf16)


def _ret_in(x, mods, layer, row, gain, w_in, cos_t, sin_t, tm):
    t = x.shape[0]
    hk = RET_HEADS * RET_QK_DIM
    hv = RET_HEADS * RET_V_DIM
    tok = lambda n: pl.BlockSpec((tm, n), lambda i: (i, 0))
    return pl.pallas_call(
        functools.partial(_ret_in_kernel, row),
        out_shape=[jax.ShapeDtypeStruct((t, hk), bf16), jax.ShapeDtypeStruct((t, hk), bf16),
                   jax.ShapeDtypeStruct((t, hv), bf16), jax.ShapeDtypeStruct((t, hv), bf16)],
        grid=(t // tm,),
        in_specs=[
            tok(D_MODEL),
            _resident((1, D_MODEL)),
            _mod_spec(layer, SHIFT_M),
            _mod_spec(layer, SCALE_M),
            _resident((D_MODEL, 2 * hk + 2 * hv)),
            tok(RET_QK_DIM),
            tok(RET_QK_DIM),
        ],
        out_specs=[tok(hk), tok(hk), tok(hv), tok(hv)],
        compiler_params=_params("arbitrary"),
        name="ret_in",
    )(x, gain, mods, mods, w_in, cos_t, sin_t)


def _log_sigmoid(v):
    return jnp.minimum(v, 0.0) - jnp.log1p(jnp.exp(-jnp.abs(v)))


def _ret_state_kernel(n_chunks, kf_ref, vf_ref, kb_ref, vb_ref, decf_ref, decb_ref, s0f_ref, s0b_ref,
                      sfall_ref, sball_ref, sf_ref, sb_ref):
    i = pl.program_id(0)
    c = RET_CHUNK

    @pl.when(i == 0)
    def _():
        sf_ref[...] = s0f_ref[...]
        sb_ref[...] = s0b_ref[...]

    idx = lax.broadcasted_iota(jnp.int32, (c, 1), 0).astype(f32)
    for h in range(RET_HEADS):
        lg_f = _log_sigmoid(decf_ref[h, 0:1, :])
        lg_b = _log_sigmoid(decb_ref[h, 0:1, :])
        ks = slice(h * RET_QK_DIM, (h + 1) * RET_QK_DIM)
        vs = slice(h * RET_V_DIM, (h + 1) * RET_V_DIM)
        kd = jnp.exp(lg_f[:, 0:1] * (c - 1.0 - idx))
        kh = (kf_ref[:, ks].astype(f32) * kd).astype(bf16)
        a = lax.dot_general(kh, vf_ref[:, vs], (((0,), (0,)), ((), ())), preferred_element_type=f32)
        s = sf_ref[h]
        sfall_ref[h] = s.astype(bf16)
        sf_ref[h] = s * jnp.exp(lg_f * c) + a
        kd = jnp.exp(lg_b[:, 0:1] * idx)
        kh = (kb_ref[:, ks].astype(f32) * kd).astype(bf16)
        a = lax.dot_general(kh, vb_ref[:, vs], (((0,), (0,)), ((), ())), preferred_element_type=f32)
        s = sb_ref[h]
        sball_ref[h] = s.astype(bf16)
        sb_ref[h] = s * jnp.exp(lg_b * c) + a


def _ret_state(k, v, dec_f, dec_b, s0f, s0b):
    t = k.shape[0]
    n = t // RET_CHUNK
    hk = RET_HEADS * RET_QK_DIM
    hv = RET_HEADS * RET_V_DIM
    st = (RET_HEADS, RET_QK_DIM, RET_V_DIM)
    fwd = lambda w: pl.BlockSpec((RET_CHUNK, w), lambda i: (i, 0))
    bwd = lambda w: pl.BlockSpec((RET_CHUNK, w), lambda i: (n - 1 - i, 0))
    return pl.pallas_call(
        functools.partial(_ret_state_kernel, n),
        out_shape=[jax.ShapeDtypeStruct((n,) + st, bf16), jax.ShapeDtypeStruct((n,) + st, bf16),
                   jax.ShapeDtypeStruct(st, f32), jax.ShapeDtypeStruct(st, f32)],
        grid=(n,),
        in_specs=[fwd(hk), fwd(hv), bwd(hk), bwd(hv),
                  _resident((RET_HEADS, 8, RET_V_DIM)), _resident((RET_HEADS, 8, RET_V_DIM)),
                  _resident(st), _resident(st)],
        out_specs=[pl.BlockSpec((None,) + st, lambda i: (i, 0, 0, 0)),
                   pl.BlockSpec((None,) + st, lambda i: (n - 1 - i, 0, 0, 0)),
                   pl.BlockSpec(st, lambda i: (0, 0, 0)),
                   pl.BlockSpec(st, lambda i: (0, 0, 0))],
        compiler_params=_params("arbitrary"),
        name="ret_state",
    )(k, v, k, v, dec_f, dec_b, s0f, s0b)


def _ret_out_kernel(row, q_ref, k_ref, v_ref, sg_ref, sf_ref, sb_ref, decf_ref, decb_ref, w_ref,
                    x_ref, gate_ref, o_ref, dec_ref, y_ref):
    c = RET_CHUNK

    @pl.when(pl.program_id(0) == 0)
    def _():
        t_i = lax.broadcasted_iota(jnp.int32, (c, c), 0)
        m_i = lax.broadcasted_iota(jnp.int32, (c, c), 1)
        rel = (t_i - m_i).astype(f32)
        for h in range(RET_HEADS):
            lg_f = _log_sigmoid(decf_ref[h, 0:1, :c])
            lg_b = _log_sigmoid(decb_ref[h, 0:1, :c])
            d_f = jnp.where(rel >= 0, jnp.exp(lg_f * jnp.maximum(rel, 0.0)), 0.0)
            d_b = jnp.where(rel <= 0, jnp.exp(lg_b * jnp.maximum(-rel, 0.0)), 0.0)
            dec_ref[h] = d_f + d_b

    idx = lax.broadcasted_iota(jnp.int32, (c, 1), 0).astype(f32)
    for h in range(RET_HEADS):
        lg_f = _log_sigmoid(decf_ref[h, 0:1, :])
        lg_b = _log_sigmoid(decb_ref[h, 0:1, :])
        q = q_ref[:, h * RET_QK_DIM:(h + 1) * RET_QK_DIM]
        k = k_ref[:, h * RET_QK_DIM:(h + 1) * RET_QK_DIM]
        v = v_ref[:, h * RET_V_DIM:(h + 1) * RET_V_DIM]
        s = lax.dot_general(q, k, (((1,), (1,)), ((), ())), preferred_element_type=f32)
        o = jnp.dot((s * dec_ref[h]).astype(bf16), v, preferred_element_type=f32)
        o = o + jnp.exp(lg_f[:, 0:1] * (idx + 1.0)) * jnp.dot(q, sf_ref[h], preferred_element_type=f32)
        o = o + jnp.exp(lg_b[:, 0:1] * (c - idx)) * jnp.dot(q, sb_ref[h], preferred_element_type=f32)
        o = o * lax.rsqrt(jnp.mean(o * o, axis=-1, keepdims=True) + NORM_EPS)
        sg = sg_ref[:, h * RET_V_DIM:(h + 1) * RET_V_DIM].astype(f32)
        y_ref[:, h * RET_V_DIM:(h + 1) * RET_V_DIM] = (sg * o).astype(bf16)
    out = jnp.dot(y_ref[...], w_ref[...], preferred_element_type=f32)
    o_ref[...] = x_ref[...] + gate_ref[row:row + 1, :] * out


def _ret_out(q, k, v, sg, sf_all, sb_all, dec_f, dec_b, w_out, x, mods, layer, row):
    t = x.shape[0]
    n = t // RET_CHUNK
    hk = RET_HEADS * RET_QK_DIM
    hv = RET_HEADS * RET_V_DIM
    st = (RET_HEADS, RET_QK_DIM, RET_V_DIM)
    tok = lambda w: pl.BlockSpec((RET_CHUNK, w), lambda i: (i, 0))
    return pl.pallas_call(
        functools.partial(_ret_out_kernel, row),
        out_shape=jax.ShapeDtypeStruct((t, D_MODEL), f32),
        grid=(n,),
        in_specs=[tok(hk), tok(hk), tok(hv), tok(hv),
                  pl.BlockSpec((None,) + st, lambda i: (i, 0, 0, 0)),
                  pl.BlockSpec((None,) + st, lambda i: (i, 0, 0, 0)),
                  _resident((RET_HEADS, 8, RET_V_DIM)), _resident((RET_HEADS, 8, RET_V_DIM)),
                  _resident((hv, D_MODEL)),
                  tok(D_MODEL),
                  _mod_spec(layer, GATE_M)],
        out_specs=tok(D_MODEL),
        scratch_shapes=[pltpu.VMEM((RET_HEADS, RET_CHUNK, RET_CHUNK), f32),
                        pltpu.VMEM((RET_CHUNK, hv), bf16)],
        compiler_params=_params("arbitrary"),
        name="ret_out",
    )(q, k, v, sg, sf_all, sb_all, dec_f, dec_b, w_out, x, mods)


def _att_in_kernel(row, x_ref, g_ref, sh_ref, sc_ref, w_ref, qg_ref, kg_ref, cos_ref, sin_ref,
                   q_ref, k_ref, v_ref):
    h = _modulated(x_ref[...], g_ref[...], sc_ref[row:row + 1, :], sh_ref[row:row + 1, :]).astype(bf16)
    nq = ATT_Q_HEADS * LANES
    nk = ATT_KV_HEADS * LANES
    lane = lax.broadcasted_iota(jnp.int32, (1, LANES), 1)
    first_half = (lane % (ATT_HEAD_DIM // 2)) < (ATT_HEAD_DIM // 4)

    def head_norm_rope(y, gain, scale):
        ms = jnp.sum(y * y, axis=-1, keepdims=True) * (1.0 / ATT_HEAD_DIM)
        yn = y * lax.rsqrt(ms + NORM_EPS) * gain
        q4 = ATT_HEAD_DIM // 4
        partner = jnp.where(first_half, pltpu.roll(yn, LANES - q4, axis=1), pltpu.roll(yn, q4, axis=1))
        return (yn * cos_ref[...] + partner * sin_ref[...]) * scale

    q = jnp.dot(h, w_ref[:, 0:nq], preferred_element_type=f32)
    for hd in range(ATT_Q_HEADS):
        sl = slice(hd * LANES, (hd + 1) * LANES)
        q_ref[:, sl] = head_norm_rope(q[:, sl], qg_ref[...], ATT_HEAD_DIM ** -0.5).astype(bf16)
    k = jnp.dot(h, w_ref[:, nq:nq + nk], preferred_element_type=f32)
    for hd in range(ATT_KV_HEADS):
        sl = slice(hd * LANES, (hd + 1) * LANES)
        k_ref[:, sl] = head_norm_rope(k[:, sl], kg_ref[...], 1.0).astype(bf16)
    v_ref[...] = jnp.dot(h, w_ref[:, nq + nk:], preferred_element_type=f32).astype(bf16)


def _att_in(x, mods, layer, row, gain, w_qkv_p, q_gain_p, k_gain_p, cos_t, sin_t, tm):
    t = x.shape[0]
    nq = ATT_Q_HEADS * LANES
    nk = ATT_KV_HEADS * LANES
    tok = lambda n: pl.BlockSpec((tm, n), lambda i: (i, 0))
    return pl.pallas_call(
        functools.partial(_att_in_kernel, row),
        out_shape=[jax.ShapeDtypeStruct((t, nq), bf16), jax.ShapeDtypeStruct((t, nk), bf16),
                   jax.ShapeDtypeStruct((t, nk), bf16)],
        grid=(t // tm,),
        in_specs=[
            tok(D_MODEL),
            _resident((1, D_MODEL)),
            _mod_spec(layer, SHIFT_M),
            _mod_spec(layer, SCALE_M),
            _resident((D_MODEL, nq + 2 * nk)),
            _resident((1, LANES)),
            _resident((1, LANES)),
            tok(LANES),
            tok(LANES),
        ],
        out_specs=[tok(nq), tok(nk), tok(nk)],
        compiler_params=_params("arbitrary"),
        name="att_in",
    )(x, gain, mods, mods, w_qkv_p, q_gain_p, k_gain_p, cos_t, sin_t)


def _att_kernel(row, n_blocks, band, *refs):
    if band:
        (sink_ref, q_ref, kc_ref, vc_ref, kp_ref, kn_ref, kx_ref, vp_ref, vn_ref, vx_ref,
         w_ref, x_ref, gate_ref, o_ref, y_ref) = refs
    else:
        sink_ref, q_ref, kc_ref, vc_ref, w_ref, x_ref, gate_ref, o_ref, y_ref = refs
    i = pl.program_id(0)
    c = ATT_BLOCK
    g = ATT_GROUP
    n_ctx = kc_ref.shape[0]
    if band:
        a = lax.broadcasted_iota(jnp.int32, (g * c, 3 * c), 0) % c
        j = lax.broadcasted_iota(jnp.int32, (g * c, 3 * c), 1)
        mask = jnp.abs(j - c - a) <= ATT_WINDOW
        mask = mask & ((j >= c) | (i > 0)) & ((j < 2 * c) | (i < n_blocks - 1))
    for kh in range(ATT_KV_HEADS):
        sl = slice(kh * LANES, (kh + 1) * LANES)
        q = jnp.concatenate([q_ref[:, (kh * g + gg) * LANES:(kh * g + gg + 1) * LANES] for gg in range(g)], axis=0)
        sink = jnp.concatenate([jnp.full((c, 1), sink_ref[kh * g + gg], f32) for gg in range(g)], axis=0)
        if band:
            keys = jnp.concatenate([kc_ref[:, sl], kp_ref[:, sl], kx_ref[:, sl], kn_ref[:, sl]], axis=0)
            vals = jnp.concatenate([vc_ref[:, sl], vp_ref[:, sl], vx_ref[:, sl], vn_ref[:, sl]], axis=0)
        else:
            keys = kc_ref[:, sl]
            vals = vc_ref[:, sl]
        s = lax.dot_general(q, keys, (((1,), (1,)), ((), ())), preferred_element_type=f32)
        if band:
            s = jnp.concatenate([s[:, :n_ctx], jnp.where(mask, s[:, n_ctx:], NEG_INF)], axis=1)
        m = jnp.maximum(jnp.max(s, axis=-1, keepdims=True), sink)
        e = jnp.exp(s - m)
        denom = jnp.sum(e, axis=-1, keepdims=True) + jnp.exp(sink - m)
        o = jnp.dot(e.astype(bf16), vals, preferred_element_type=f32) / denom
        for gg in range(g):
            hd = kh * g + gg
            y_ref[:, hd * LANES:(hd + 1) * LANES] = o[gg * c:(gg + 1) * c, :].astype(bf16)
    out = jnp.dot(y_ref[...], w_ref[...], preferred_element_type=f32)
    o_ref[...] = x_ref[...] + gate_ref[row:row + 1, :] * out


def _attention(sink, q, kc, vc, kx, vx, w_o_p, x, mods, layer, row):
    t = x.shape[0]
    c = ATT_BLOCK
    n = t // c
    nq = ATT_Q_HEADS * LANES
    nk = ATT_KV_HEADS * LANES
    band = kx is not None
    tok = lambda w: pl.BlockSpec((c, w), lambda i: (i, 0))
    prev = pl.BlockSpec((c, nk), lambda i: (jnp.maximum(i - 1, 0), 0))
    nxt = pl.BlockSpec((c, nk), lambda i: (jnp.minimum(i + 1, n - 1), 0))
    in_specs = [pl.BlockSpec(memory_space=pltpu.SMEM), tok(nq),
                _resident(kc.shape), _resident(vc.shape)]
    args = [sink, q, kc, vc]
    if band:
        in_specs += [prev, nxt, tok(nk), prev, nxt, tok(nk)]
        args += [kx, kx, kx, vx, vx, vx]
    in_specs += [_resident((nq, D_MODEL)), tok(D_MODEL), _mod_spec(layer, GATE_M)]
    args += [w_o_p, x, mods]
    return pl.pallas_call(
        functools.partial(_att_kernel, row, n, band),
        out_shape=jax.ShapeDtypeStruct((t, D_MODEL), f32),
        grid=(n,),
        in_specs=in_specs,
        out_specs=tok(D_MODEL),
        scratch_shapes=[pltpu.VMEM((c, nq), bf16)],
        compiler_params=_params("arbitrary"),
        name="att_band" if band else "att_ctx",
    )(*args)


def _ffn_kernel(row, x_ref, g_ref, sh_ref, sc_ref, gate_ref, wgu_ref, wd_ref, o_ref):
    x = x_ref[...]
    h = _modulated(x, g_ref[...], sc_ref[row:row + 1, :], sh_ref[row:row + 1, :]).astype(bf16)
    acc = jnp.zeros(x.shape, f32)
    for c0 in range(0, FFN_HIDDEN, FFN_CHUNK):
        a = jnp.dot(h, wgu_ref[:, c0:c0 + FFN_CHUNK], preferred_element_type=f32)
        b = jnp.dot(h, wgu_ref[:, FFN_HIDDEN + c0:FFN_HIDDEN + c0 + FFN_CHUNK], preferred_element_type=f32)
        act = (_silu(a) * b).astype(bf16)
        acc = acc + jnp.dot(act, wd_ref[c0:c0 + FFN_CHUNK, :], preferred_element_type=f32)
    o_ref[...] = x + gate_ref[row:row + 1, :] * acc


def _ffn(x, mods, layer, row, gain, w_gu, w_down, tm):
    t = x.shape[0]
    tok = pl.BlockSpec((tm, D_MODEL), lambda i: (i, 0))
    return pl.pallas_call(
        functools.partial(_ffn_kernel, row),
        out_shape=jax.ShapeDtypeStruct((t, D_MODEL), f32),
        grid=(t // tm,),
        in_specs=[tok, _resident((1, D_MODEL)),
                  _mod_spec(layer, SHIFT_F), _mod_spec(layer, SCALE_F), _mod_spec(layer, GATE_F),
                  _resident((D_MODEL, 2 * FFN_HIDDEN)), _resident((FFN_HIDDEN, D_MODEL))],
        out_specs=tok,
        compiler_params=_params("arbitrary"),
        name="ffn",
    )(x, gain, mods, mods, mods, w_gu, w_down)


def _rope_tables(t, quarter, width):
    n_rows = t // GRID_W
    rows = jnp.repeat(jnp.arange(n_rows), GRID_W).astype(f32)
    cols = (jnp.arange(n_rows * GRID_W) % GRID_W).astype(f32)
    inv = ROPE_BASE ** (-jnp.arange(quarter, dtype=f32) / quarter)
    ar, ac = rows[:, None] * inv, cols[:, None] * inv
    pad = jnp.zeros((t, width - 4 * quarter), f32)
    cos_t = jnp.concatenate([jnp.cos(ar), jnp.cos(ar), jnp.cos(ac), jnp.cos(ac), pad], axis=1)
    sin_t = jnp.concatenate([-jnp.sin(ar), jnp.sin(ar), -jnp.sin(ac), jnp.sin(ac), pad], axis=1)
    return cos_t, sin_t


def _identity_tables(t, width):
    return jnp.ones((t, width), f32), jnp.zeros((t, width), f32)


def _pad_heads(w, n_heads):
    lead = w.shape[:-1]
    w = w.reshape(lead + (n_heads, ATT_HEAD_DIM))
    w = jnp.pad(w, [(0, 0)] * len(lead) + [(0, 0), (0, LANES - ATT_HEAD_DIM)])
    return w.reshape(lead + (n_heads * LANES,))


def kernel(x, c, ctx, c_ctx, ada_w, ada_b, norm_mix, norm_ffn, conv_w1, conv_b1, conv_dw, conv_dw_b, conv_norm, conv_w2, ret_w_in, ret_decay_f, ret_decay_b, ret_w_out, att_w_qkv, att_q_norm, att_k_norm, att_sink, att_w_o, ffn_w_gu, ffn_w_down):
    assert x.shape[0] == 1 and c.shape[0] == 1 and ctx.shape[0] == 1
    t_lat, t_ctx = x.shape[1], ctx.shape[1]
    tm_lat, tm_ctx = 512, t_ctx
    xs, hc = x[0], ctx[0]
    lat, cx = 0, 1

    cond8 = jnp.zeros((8, D_MODEL), f32).at[lat].set(c[0]).at[cx].set(c_ctx)
    mods = _adaln(cond8, ada_w, ada_b)

    row1 = lambda v: v.reshape(1, -1)
    for i in range(DEPTH):
        kind, j, last = i % N_MIXERS, i // N_MIXERS, i == DEPTH - 1
        with_ctx = not last
        g_mix = row1(norm_mix[i])
        if kind == 0:
            w1, w2 = conv_w1[j].astype(bf16), conv_w2[j].astype(bf16)
            cargs = (conv_dw[j], row1(conv_dw_b[j]), row1(conv_norm[j]), w2)
            u = _conv_in(xs, mods, i, lat, g_mix, w1, row1(conv_b1[j]), tm_lat)
            xs = _conv_out(u, xs, mods, i, lat, *cargs, tm_lat)
            if with_ctx:
                u = _conv_in(hc, mods, i, cx, g_mix, w1, row1(conv_b1[j]), tm_ctx)
                hc = _conv_out(u, hc, mods, i, cx, *cargs, tm_ctx)
        elif kind == 1:
            w_in, w_out = ret_w_in[j].astype(bf16), ret_w_out[j].astype(bf16)
            bdec = lambda d: jnp.broadcast_to(d[:, None, None], (RET_HEADS, 8, RET_V_DIM)).astype(f32)
            dec_f, dec_b = bdec(ret_decay_f[j]), bdec(ret_decay_b[j])
            zeros = jnp.zeros((RET_HEADS, RET_QK_DIM, RET_V_DIM), f32)
            qc, kc, vc, gc = _ret_in(hc, mods, i, cx, g_mix, w_in, *_identity_tables(t_ctx, RET_QK_DIM), tm_ctx)
            sf_c, sb_c, s0f, s0b = _ret_state(kc, vc, dec_f, dec_b, zeros, zeros)
            qx, kx, vx, gx = _ret_in(xs, mods, i, lat, g_mix, w_in, *_rope_tables(t_lat, RET_QK_DIM // 4, RET_QK_DIM), tm_lat)
            sf_x, sb_x, _, _ = _ret_state(kx, vx, dec_f, dec_b, s0f, s0b)
            xs = _ret_out(qx, kx, vx, gx, sf_x, sb_x, dec_f, dec_b, w_out, xs, mods, i, lat)
            if with_ctx:
                hc = _ret_out(qc, kc, vc, gc, sf_c, sb_c, dec_f, dec_b, w_out, hc, mods, i, cx)
        else:
            w_qkv = jnp.concatenate([
                _pad_heads(att_w_qkv[j][:, :ATT_Q_HEADS * ATT_HEAD_DIM], ATT_Q_HEADS),
                _pad_heads(att_w_qkv[j][:, ATT_Q_HEADS * ATT_HEAD_DIM:], 2 * ATT_KV_HEADS)], axis=1).astype(bf16)
            w_o = _pad_heads(att_w_o[j].T, ATT_Q_HEADS).T.astype(bf16)
            qg, kg = _pad_heads(row1(att_q_norm[j]), 1), _pad_heads(row1(att_k_norm[j]), 1)
            sink = att_sink[j].astype(f32)
            qc, kc, vc = _att_in(hc, mods, i, cx, g_mix, w_qkv, qg, kg, *_identity_tables(t_ctx, LANES), tm_ctx)
            qx, kx, vx = _att_in(xs, mods, i, lat, g_mix, w_qkv, qg, kg, *_rope_tables(t_lat, ATT_HEAD_DIM // 4, LANES), tm_lat)
            xs = _attention(sink, qx, kc, vc, kx, vx, w_o, xs, mods, i, lat)
            if with_ctx:
                hc = _attention(sink, qc, kc, vc, None, None, w_o, hc, mods, i, cx)
        g_ffn = row1(norm_ffn[i])
        w_gu, w_down = ffn_w_gu[i].astype(bf16), ffn_w_down[i].astype(bf16)
        xs = _ffn(xs, mods, i, lat, g_ffn, w_gu, w_down, tm_lat)
        if with_ctx:
            hc = _ffn(hc, mods, i, cx, g_ffn, w_gu, w_down, tm_ctx)
    return xs[None]
```

```python
import functools

import jax
import jax.numpy as jnp
from jax import lax
from jax.experimental import pallas as pl
from jax.experimental.pallas import tpu as pltpu

f32 = jnp.float32
bf16 = jnp.bfloat16

D_MODEL = 1024
DEPTH = 4
GRID_W = 64
N_MIXERS = 3
CONV_WIDTH = 31
CONV_HALO = 16
CONV_ROWS = 64
RET_HEADS = 4
RET_QK_DIM = 256
RET_V_DIM = 512
RET_CHUNK = 256
ATT_Q_HEADS = 16
ATT_KV_HEADS = 4
ATT_GROUP = ATT_Q_HEADS // ATT_KV_HEADS
ATT_HEAD_DIM = 64
ATT_WINDOW = 128
ATT_BLOCK = 128
ATT_ROWS = 32
FFN_HIDDEN = 2816
FFN_CHUNK = 256
ROPE_BASE = 10000.0
NORM_EPS = 1e-6
NEG_INF = -1e30
LOG2_E = 1.4426950408889634
LANES = 128
VMEM_LIMIT = 56 * 1024 * 1024

SHIFT_M, SCALE_M, GATE_M, SHIFT_F, SCALE_F, GATE_F = range(6)


def _params(*sem):
    return pltpu.CompilerParams(dimension_semantics=sem, vmem_limit_bytes=VMEM_LIMIT)


def _resident(shape):
    nd = len(shape)
    return pl.BlockSpec(shape, lambda *_: (0,) * nd, pipeline_mode=pl.Buffered(1))


def _mod_spec(layer, which):
    return pl.BlockSpec((None, 8, D_MODEL), lambda *_: (layer, 0, which))


def _modulated(x, gain, scale, shift):
    ms = jnp.mean(x * x, axis=-1, keepdims=True)
    return (x * lax.rsqrt(ms + NORM_EPS)) * (gain * (1.0 + scale)) + shift


def _silu(v):
    return v * jax.nn.sigmoid(v)


def _tile_table(row_ref, col_ref):
    return jnp.concatenate([row_ref[r:r + 1, :] + col_ref[...] for r in range(row_ref.shape[0])], axis=0)


def _table_specs(tm, width):
    row = pl.BlockSpec((tm // GRID_W, width), lambda i: (i, 0))
    col = _resident((GRID_W, width))
    return [row, col, row, col]


def _adaln_kernel(c_ref, w_ref, b_ref, o_ref):
    s = _silu(c_ref[...])
    o_ref[...] = jnp.dot(s, w_ref[...], precision=lax.Precision.HIGHEST,
                         preferred_element_type=f32) + b_ref[...]


def _adaln(cond8, ada_w, ada_b):
    tn = 1536
    n = 6 * D_MODEL
    return pl.pallas_call(
        _adaln_kernel,
        out_shape=jax.ShapeDtypeStruct((DEPTH, 8, n), f32),
        grid=(DEPTH, n // tn),
        in_specs=[
            pl.BlockSpec((8, D_MODEL), lambda l, j: (0, 0)),
            pl.BlockSpec((None, D_MODEL, tn), lambda l, j: (l, 0, j)),
            pl.BlockSpec((None, 1, tn), lambda l, j: (l, 0, j)),
        ],
        out_specs=pl.BlockSpec((None, 8, tn), lambda l, j: (l, 0, j)),
        compiler_params=_params("arbitrary", "arbitrary"),
        name="adaln",
    )(cond8, ada_w, ada_b.reshape(DEPTH, 1, n))


def _conv_in_kernel(row, x_ref, g_ref, sh_ref, sc_ref, w_ref, b_ref, o_ref):
    h = _modulated(x_ref[...], g_ref[...], sc_ref[row:row + 1, :], sh_ref[row:row + 1, :])
    y = jnp.dot(h.astype(bf16), w_ref[...], preferred_element_type=f32) + b_ref[...]
    o_ref[...] = y[:, :D_MODEL] * jax.nn.sigmoid(y[:, D_MODEL:])


def _conv_in(x, mods, layer, row, gain, w1, b1, tm):
    t = x.shape[0]
    return pl.pallas_call(
        functools.partial(_conv_in_kernel, row),
        out_shape=jax.ShapeDtypeStruct((t, D_MODEL), f32),
        grid=(t // tm,),
        in_specs=[
            pl.BlockSpec((tm, D_MODEL), lambda i: (i, 0)),
            _resident((1, D_MODEL)),
            _mod_spec(layer, SHIFT_M),
            _mod_spec(layer, SCALE_M),
            _resident((D_MODEL, 2 * D_MODEL)),
            _resident((1, 2 * D_MODEL)),
        ],
        out_specs=pl.BlockSpec((tm, D_MODEL), lambda i: (i, 0)),
        compiler_params=_params("arbitrary"),
        name="conv_in",
    )(x, gain, mods, mods, w1, b1)


def _conv_out_kernel(row, n_tiles, um_ref, up_ref, un_ref, dw_ref, dwb_ref, ng_ref, w2_ref,
                     x_ref, gate_ref, o_ref, win_ref, acc_ref, y_ref):
    i = pl.program_id(0)
    tm = um_ref.shape[0]
    rb = CONV_ROWS
    win_ref[0:CONV_HALO, :] = jnp.where(i > 0, up_ref[...], 0.0)
    win_ref[CONV_HALO:CONV_HALO + tm, :] = um_ref[...]
    win_ref[CONV_HALO + tm:, :] = jnp.where(i < n_tiles - 1, un_ref[...], 0.0)
    off = CONV_HALO - CONV_WIDTH // 2

    def body(r, carry):
        base = pl.multiple_of(r * rb, rb)
        for lb in range(D_MODEL // LANES):
            ls = slice(lb * LANES, (lb + 1) * LANES)
            win = win_ref[pl.ds(base, rb + 2 * CONV_HALO), ls]
            out = jnp.broadcast_to(dwb_ref[:, ls], (rb, LANES))
            for s in range(8):
                z = None
                for k in range(CONV_WIDTH):
                    if (off + k) % 8 != s:
                        continue
                    j = (off + k) // 8
                    term = win[8 * j:8 * j + rb + 8, :] * dw_ref[k:k + 1, ls]
                    z = term if z is None else z + term
                if z is None:
                    continue
                out = out + (z[0:rb] if s == 0 else pltpu.roll(z, rb + 8 - s, axis=0)[0:rb])
            acc_ref[:, ls] = out
        acc = acc_ref[...]
        mu = jnp.mean(acc, axis=-1, keepdims=True)
        xc = acc - mu
        var = jnp.mean(xc * xc, axis=-1, keepdims=True)
        yn = xc * lax.rsqrt(var + NORM_EPS) * ng_ref[...]
        y_ref[pl.ds(base, rb), :] = _silu(yn).astype(bf16)
        return carry

    lax.fori_loop(0, tm // rb, body, 0)
    o = jnp.dot(y_ref[...], w2_ref[...], preferred_element_type=f32)
    o_ref[...] = x_ref[...] + gate_ref[row:row + 1, :] * o


def _conv_out(u, x, mods, layer, row, dw, dw_b, norm_g, w2, tm):
    t = x.shape[0]
    n_tiles = t // tm
    hb = tm // CONV_HALO
    n_hb = t // CONV_HALO
    return pl.pallas_call(
        functools.partial(_conv_out_kernel, row, n_tiles),
        out_shape=jax.ShapeDtypeStruct((t, D_MODEL), f32),
        grid=(n_tiles,),
        in_specs=[
            pl.BlockSpec((tm, D_MODEL), lambda i: (i, 0)),
            pl.BlockSpec((CONV_HALO, D_MODEL), lambda i: (jnp.maximum(i * hb - 1, 0), 0)),
            pl.BlockSpec((CONV_HALO, D_MODEL), lambda i: (jnp.minimum((i + 1) * hb, n_hb - 1), 0)),
            _resident((CONV_WIDTH, D_MODEL)),
            _resident((1, D_MODEL)),
            _resident((1, D_MODEL)),
            _resident((D_MODEL, D_MODEL)),
            pl.BlockSpec((tm, D_MODEL), lambda i: (i, 0)),
            _mod_spec(layer, GATE_M),
        ],
        out_specs=pl.BlockSpec((tm, D_MODEL), lambda i: (i, 0)),
        scratch_shapes=[
            pltpu.VMEM((tm + 2 * CONV_HALO, D_MODEL), f32),
            pltpu.VMEM((CONV_ROWS, D_MODEL), f32),
            pltpu.VMEM((tm, D_MODEL), bf16),
        ],
        compiler_params=_params("arbitrary"),
        name="conv_out",
    )(u, u, u, dw, dw_b, norm_g, w2, x, mods)


def _ret_in_kernel(row, x_ref, g_ref, sh_ref, sc_ref, w_ref, rcos_ref, ccos_ref, rsin_ref, csin_ref,
                   q_ref, k_ref, v_ref, sg_ref):
    h = _modulated(x_ref[...], g_ref[...], sc_ref[row:row + 1, :], sh_ref[row:row + 1, :]).astype(bf16)
    hk = RET_HEADS * RET_QK_DIM
    hv = RET_HEADS * RET_V_DIM
    cos_t = _tile_table(rcos_ref, ccos_ref)
    sin_t = _tile_table(rsin_ref, csin_ref)

    def rope(y, scale):
        outs = []
        for b in range(hk // LANES):
            yb = y[:, b * LANES:(b + 1) * LANES]
            tb = (b % 2) * LANES
            rot = yb * cos_t[:, tb:tb + LANES] + pltpu.roll(yb, LANES // 2, axis=1) * sin_t[:, tb:tb + LANES]
            outs.append(rot * scale)
        return jnp.concatenate(outs, axis=1)

    q = jnp.dot(h, w_ref[:, 0:hk], preferred_element_type=f32)
    q_ref[...] = rope(q, 1.0).astype(bf16)
    k = jnp.dot(h, w_ref[:, hk:2 * hk], preferred_element_type=f32)
    k_ref[...] = rope(k, RET_QK_DIM ** -0.5).astype(bf16)
    v_ref[...] = jnp.dot(h, w_ref[:, 2 * hk:2 * hk + hv], preferred_element_type=f32).astype(bf16)
    g = jnp.dot(h, w_ref[:, 2 * hk + hv:], preferred_element_type=f32)
    sg_ref[...] = _silu(g).astype(bf16)


def _ret_in(x, mods, layer, row, gain, w_in, tabs, tm):
    t = x.shape[0]
    hk = RET_HEADS * RET_QK_DIM
    hv = RET_HEADS * RET_V_DIM
    tok = lambda n: pl.BlockSpec((tm, n), lambda i: (i, 0))
    return pl.pallas_call(
        functools.partial(_ret_in_kernel, row),
        out_shape=[jax.ShapeDtypeStruct((t, hk), bf16), jax.ShapeDtypeStruct((t, hk), bf16),
                   jax.ShapeDtypeStruct((t, hv), bf16), jax.ShapeDtypeStruct((t, hv), bf16)],
        grid=(t // tm,),
        in_specs=[
            tok(D_MODEL),
            _resident((1, D_MODEL)),
            _mod_spec(layer, SHIFT_M),
            _mod_spec(layer, SCALE_M),
            _resident((D_MODEL, 2 * hk + 2 * hv)),
            *_table_specs(tm, RET_QK_DIM),
        ],
        out_specs=[tok(hk), tok(hk), tok(hv), tok(hv)],
        compiler_params=_params("arbitrary"),
        name="ret_in",
    )(x, gain, mods, mods, w_in, *tabs)


def _log_sigmoid(v):
    return jnp.minimum(v, 0.0) - jnp.log1p(jnp.exp(-jnp.abs(v)))


def _ret_state_kernel(n_chunks, kf_ref, vf_ref, kb_ref, vb_ref, decf_ref, decb_ref, s0f_ref, s0b_ref,
                      sfall_ref, sball_ref, sf_ref, sb_ref):
    i = pl.program_id(0)
    c = RET_CHUNK

    @pl.when(i == 0)
    def _():
        sf_ref[...] = s0f_ref[...]
        sb_ref[...] = s0b_ref[...]

    idx = lax.broadcasted_iota(jnp.int32, (c, 1), 0).astype(f32)
    for h in range(RET_HEADS):
        lg_f = _log_sigmoid(decf_ref[h, 0:1, :])
        lg_b = _log_sigmoid(decb_ref[h, 0:1, :])
        ks = slice(h * RET_QK_DIM, (h + 1) * RET_QK_DIM)
        vs = slice(h * RET_V_DIM, (h + 1) * RET_V_DIM)
        kd = jnp.exp(lg_f[:, 0:1] * (c - 1.0 - idx))
        kh = (kf_ref[:, ks].astype(f32) * kd).astype(bf16)
        a = lax.dot_general(kh, vf_ref[:, vs], (((0,), (0,)), ((), ())), preferred_element_type=f32)
        s = sf_ref[h]
        sfall_ref[h] = s.astype(bf16)
        sf_ref[h] = s * jnp.exp(lg_f * c) + a
        kd = jnp.exp(lg_b[:, 0:1] * idx)
        kh = (kb_ref[:, ks].astype(f32) * kd).astype(bf16)
        a = lax.dot_general(kh, vb_ref[:, vs], (((0,), (0,)), ((), ())), preferred_element_type=f32)
        s = sb_ref[h]
        sball_ref[h] = s.astype(bf16)
        sb_ref[h] = s * jnp.exp(lg_b * c) + a


def _ret_state(k, v, dec_f, dec_b, s0f, s0b):
    t = k.shape[0]
    n = t // RET_CHUNK
    hk = RET_HEADS * RET_QK_DIM
    hv = RET_HEADS * RET_V_DIM
    st = (RET_HEADS, RET_QK_DIM, RET_V_DIM)
    fwd = lambda w: pl.BlockSpec((RET_CHUNK, w), lambda i: (i, 0))
    bwd = lambda w: pl.BlockSpec((RET_CHUNK, w), lambda i: (n - 1 - i, 0))
    return pl.pallas_call(
        functools.partial(_ret_state_kernel, n),
        out_shape=[jax.ShapeDtypeStruct((n,) + st, bf16), jax.ShapeDtypeStruct((n,) + st, bf16),
                   jax.ShapeDtypeStruct(st, f32), jax.ShapeDtypeStruct(st, f32)],
        grid=(n,),
        in_specs=[fwd(hk), fwd(hv), bwd(hk), bwd(hv),
                  _resident((RET_HEADS, 8, RET_V_DIM)), _resident((RET_HEADS, 8, RET_V_DIM)),
                  _resident(st), _resident(st)],
        out_specs=[pl.BlockSpec((None,) + st, lambda i: (i, 0, 0, 0)),
                   pl.BlockSpec((None,) + st, lambda i: (n - 1 - i, 0, 0, 0)),
                   pl.BlockSpec(st, lambda i: (0, 0, 0)),
                   pl.BlockSpec(st, lambda i: (0, 0, 0))],
        compiler_params=_params("arbitrary"),
        name="ret_state",
    )(k, v, k, v, dec_f, dec_b, s0f, s0b)


def _ret_out_kernel(row, q_ref, k_ref, v_ref, sg_ref, sf_ref, sb_ref, decf_ref, decb_ref, w_ref,
                    x_ref, gate_ref, o_ref, dec_ref, y_ref):
    c = RET_CHUNK

    @pl.when(pl.program_id(0) == 0)
    def _():
        t_i = lax.broadcasted_iota(jnp.int32, (c, c), 0)
        m_i = lax.broadcasted_iota(jnp.int32, (c, c), 1)
        rel = (t_i - m_i).astype(f32)
        for h in range(RET_HEADS):
            lg_f = _log_sigmoid(decf_ref[h, 0:1, :c])
            lg_b = _log_sigmoid(decb_ref[h, 0:1, :c])
            d_f = jnp.where(rel >= 0, jnp.exp(lg_f * jnp.maximum(rel, 0.0)), 0.0)
            d_b = jnp.where(rel <= 0, jnp.exp(lg_b * jnp.maximum(-rel, 0.0)), 0.0)
            dec_ref[h] = d_f + d_b

    idx = lax.broadcasted_iota(jnp.int32, (c, 1), 0).astype(f32)
    for h in range(RET_HEADS):
        lg_f = _log_sigmoid(decf_ref[h, 0:1, :])
        lg_b = _log_sigmoid(decb_ref[h, 0:1, :])
        q = q_ref[:, h * RET_QK_DIM:(h + 1) * RET_QK_DIM]
        k = k_ref[:, h * RET_QK_DIM:(h + 1) * RET_QK_DIM]
        v = v_ref[:, h * RET_V_DIM:(h + 1) * RET_V_DIM]
        s = lax.dot_general(q, k, (((1,), (1,)), ((), ())), preferred_element_type=f32)
        o = jnp.dot((s * dec_ref[h]).astype(bf16), v, preferred_element_type=f32)
        o = o + jnp.exp(lg_f[:, 0:1] * (idx + 1.0)) * jnp.dot(q, sf_ref[h], preferred_element_type=f32)
        o = o + jnp.exp(lg_b[:, 0:1] * (c - idx)) * jnp.dot(q, sb_ref[h], preferred_element_type=f32)
        o = o * lax.rsqrt(jnp.mean(o * o, axis=-1, keepdims=True) + NORM_EPS)
        sg = sg_ref[:, h * RET_V_DIM:(h + 1) * RET_V_DIM].astype(f32)
        y_ref[:, h * RET_V_DIM:(h + 1) * RET_V_DIM] = (sg * o).astype(bf16)
    out = jnp.dot(y_ref[...], w_ref[...], preferred_element_type=f32)
    o_ref[...] = x_ref[...] + gate_ref[row:row + 1, :] * out


def _ret_out(q, k, v, sg, sf_all, sb_all, dec_f, dec_b, w_out, x, mods, layer, row):
    t = x.shape[0]
    n = t // RET_CHUNK
    hk = RET_HEADS * RET_QK_DIM
    hv = RET_HEADS * RET_V_DIM
    st = (RET_HEADS, RET_QK_DIM, RET_V_DIM)
    tok = lambda w: pl.BlockSpec((RET_CHUNK, w), lambda i: (i, 0))
    return pl.pallas_call(
        functools.partial(_ret_out_kernel, row),
        out_shape=jax.ShapeDtypeStruct((t, D_MODEL), f32),
        grid=(n,),
        in_specs=[tok(hk), tok(hk), tok(hv), tok(hv),
                  pl.BlockSpec((None,) + st, lambda i: (i, 0, 0, 0)),
                  pl.BlockSpec((None,) + st, lambda i: (i, 0, 0, 0)),
                  _resident((RET_HEADS, 8, RET_V_DIM)), _resident((RET_HEADS, 8, RET_V_DIM)),
                  _resident((hv, D_MODEL)),
                  tok(D_MODEL),
                  _mod_spec(layer, GATE_M)],
        out_specs=tok(D_MODEL),
        scratch_shapes=[pltpu.VMEM((RET_HEADS, RET_CHUNK, RET_CHUNK), f32),
                        pltpu.VMEM((RET_CHUNK, hv), bf16)],
        compiler_params=_params("arbitrary"),
        name="ret_out",
    )(q, k, v, sg, sf_all, sb_all, dec_f, dec_b, w_out, x, mods)


def _att_in_kernel(row, x_ref, g_ref, sh_ref, sc_ref, w_ref, qg_ref, kg_ref,
                   rcos_ref, ccos_ref, rsin_ref, csin_ref, q_ref, k_ref, v_ref):
    h = _modulated(x_ref[...], g_ref[...], sc_ref[row:row + 1, :], sh_ref[row:row + 1, :]).astype(bf16)
    nq = ATT_Q_HEADS * LANES
    nk = ATT_KV_HEADS * LANES
    cos_t = _tile_table(rcos_ref, ccos_ref)
    sin_t = _tile_table(rsin_ref, csin_ref)

    def partner(t):
        return pltpu.roll(t, LANES // 2, axis=1)

    def tables(gain_ref, scale):
        gain = jnp.broadcast_to(gain_ref[...], (8, LANES))
        return cos_t * (gain[0:1, :] * scale), sin_t * (partner(gain)[0:1, :] * scale)

    def head_norm_rope(y, cos_g, sin_g):
        ms = jnp.sum(y * y, axis=-1, keepdims=True) * (1.0 / ATT_HEAD_DIM)
        return (y * cos_g + partner(y) * sin_g) * lax.rsqrt(ms + NORM_EPS)

    q_tabs = tables(qg_ref, ATT_HEAD_DIM ** -0.5 * LOG2_E)
    k_tabs = tables(kg_ref, 1.0)
    for c0 in range(0, nq + nk, 2 * LANES):
        y = jnp.dot(h, w_ref[:, c0:c0 + 2 * LANES], preferred_element_type=f32)
        for col in (c0, c0 + LANES):
            yh = y[:, col - c0:col - c0 + LANES]
            if col < nq:
                q_ref[:, col:col + LANES] = head_norm_rope(yh, *q_tabs).astype(bf16)
            else:
                k_ref[:, col - nq:col - nq + LANES] = head_norm_rope(yh, *k_tabs).astype(bf16)
    v_ref[...] = jnp.dot(h, w_ref[:, nq + nk:], preferred_element_type=f32).astype(bf16)


def _att_in(x, mods, layer, row, gain, w_qkv_p, q_gain_p, k_gain_p, tabs, tm):
    t = x.shape[0]
    nq = ATT_Q_HEADS * LANES
    nk = ATT_KV_HEADS * LANES
    tok = lambda n: pl.BlockSpec((tm, n), lambda i: (i, 0))
    return pl.pallas_call(
        functools.partial(_att_in_kernel, row),
        out_shape=[jax.ShapeDtypeStruct((t, nq), bf16), jax.ShapeDtypeStruct((t, nk), bf16),
                   jax.ShapeDtypeStruct((t, nk), bf16)],
        grid=(t // tm,),
        in_specs=[
            tok(D_MODEL),
            _resident((1, D_MODEL)),
            _mod_spec(layer, SHIFT_M),
            _mod_spec(layer, SCALE_M),
            _resident((D_MODEL, nq + 2 * nk)),
            _resident((1, LANES)),
            _resident((1, LANES)),
            *_table_specs(tm, LANES),
        ],
        out_specs=[tok(nq), tok(nk), tok(nk)],
        compiler_params=_params("arbitrary"),
        name="att_in",
    )(x, gain, mods, mods, w_qkv_p, q_gain_p, k_gain_p, *tabs)


def _att_kernel(row, n_blocks, band, *refs):
    if band:
        (sink_ref, q_ref, kc_ref, vc_ref, kp_ref, kn_ref, kx_ref, vp_ref, vn_ref, vx_ref,
         w_ref, x_ref, gate_ref, o_ref, s_ref, e_ref, l_ref, y_ref) = refs
    else:
        sink_ref, q_ref, kc_ref, vc_ref, w_ref, x_ref, gate_ref, o_ref, s_ref, e_ref, l_ref, y_ref = refs
    i = pl.program_id(0)
    c = ATT_BLOCK
    g = ATT_GROUP
    rc = ATT_ROWS
    n_ctx = kc_ref.shape[0]
    if band:
        a = lax.broadcasted_iota(jnp.int32, (c, c), 0)
        j = lax.broadcasted_iota(jnp.int32, (c, c), 1)
        prev_ok = (j >= a + (c - ATT_WINDOW)) & (i > 0)
        next_ok = (j <= a + (ATT_WINDOW - c)) & (i < n_blocks - 1)
    for kh in range(ATT_KV_HEADS):
        sl = slice(kh * LANES, (kh + 1) * LANES)
        q = jnp.concatenate([q_ref[:, (kh * g + gg) * LANES:(kh * g + gg + 1) * LANES] for gg in range(g)], axis=0)
        if band:
            keys = jnp.concatenate([kc_ref[:, sl], kp_ref[:, sl], kx_ref[:, sl], kn_ref[:, sl]], axis=0)
            vals = jnp.concatenate([vc_ref[:, sl], vp_ref[:, sl], vx_ref[:, sl], vn_ref[:, sl]], axis=0)
        else:
            keys = kc_ref[:, sl]
            vals = vc_ref[:, sl]
        s_ref[kh] = lax.dot_general(q, keys, (((1,), (1,)), ((), ())), preferred_element_type=f32)
        for gg in range(g):
            sink = sink_ref[kh * g + gg] * LOG2_E
            for r0 in range(0, c, rc):
                rows = slice(gg * c + r0, gg * c + r0 + rc)
                s = s_ref[kh, rows, :]
                if band:
                    s = jnp.concatenate([
                        s[:, :n_ctx],
                        jnp.where(prev_ok[r0:r0 + rc, :], s[:, n_ctx:n_ctx + c], NEG_INF),
                        s[:, n_ctx + c:n_ctx + 2 * c],
                        jnp.where(next_ok[r0:r0 + rc, :], s[:, n_ctx + 2 * c:], NEG_INF)], axis=1)
                m = jnp.maximum(jnp.max(s, axis=-1, keepdims=True), sink)
                e = jnp.exp2(s - m)
                e_ref[kh, rows, :] = e.astype(bf16)
                l_ref[kh, rows, :] = 1.0 / (jnp.sum(e, axis=-1, keepdims=True) + jnp.exp2(sink - m))
        o = jnp.dot(e_ref[kh], vals, preferred_element_type=f32) * l_ref[kh]
        for gg in range(g):
            hd = kh * g + gg
            y_ref[:, hd * LANES:(hd + 1) * LANES] = o[gg * c:(gg + 1) * c, :].astype(bf16)
    out = jnp.dot(y_ref[...], w_ref[...], preferred_element_type=f32)
    o_ref[...] = x_ref[...] + gate_ref[row:row + 1, :] * out


def _attention(sink, q, kc, vc, kx, vx, w_o_p, x, mods, layer, row):
    t = x.shape[0]
    c = ATT_BLOCK
    n = t // c
    nq = ATT_Q_HEADS * LANES
    nk = ATT_KV_HEADS * LANES
    band = kx is not None
    n_keys = kc.shape[0] + (3 * c if band else 0)
    tok = lambda w: pl.BlockSpec((c, w), lambda i: (i, 0))
    prev = pl.BlockSpec((c, nk), lambda i: (jnp.maximum(i - 1, 0), 0))
    nxt = pl.BlockSpec((c, nk), lambda i: (jnp.minimum(i + 1, n - 1), 0))
    in_specs = [pl.BlockSpec(memory_space=pltpu.SMEM), tok(nq),
                _resident(kc.shape), _resident(vc.shape)]
    args = [sink, q, kc, vc]
    if band:
        in_specs += [prev, nxt, tok(nk), prev, nxt, tok(nk)]
        args += [kx, kx, kx, vx, vx, vx]
    in_specs += [_resident((nq, D_MODEL)), tok(D_MODEL), _mod_spec(layer, GATE_M)]
    args += [w_o_p, x, mods]
    return pl.pallas_call(
        functools.partial(_att_kernel, row, n, band),
        out_shape=jax.ShapeDtypeStruct((t, D_MODEL), f32),
        grid=(n,),
        in_specs=in_specs,
        out_specs=tok(D_MODEL),
        scratch_shapes=[pltpu.VMEM((ATT_KV_HEADS, ATT_GROUP * c, n_keys), f32),
                        pltpu.VMEM((ATT_KV_HEADS, ATT_GROUP * c, n_keys), bf16),
                        pltpu.VMEM((ATT_KV_HEADS, ATT_GROUP * c, 1), f32),
                        pltpu.VMEM((c, nq), bf16)],
        compiler_params=_params("arbitrary"),
        name="att_band" if band else "att_ctx",
    )(*args)


def _ffn_kernel(row, x_ref, g_ref, sh_ref, sc_ref, gate_ref, wgu_ref, wd_ref, o_ref):
    x = x_ref[...]
    h = _modulated(x, g_ref[...], sc_ref[row:row + 1, :], sh_ref[row:row + 1, :]).astype(bf16)
    acc = jnp.zeros(x.shape, f32)
    for c0 in range(0, FFN_HIDDEN, FFN_CHUNK):
        a = jnp.dot(h, wgu_ref[:, c0:c0 + FFN_CHUNK], preferred_element_type=f32)
        b = jnp.dot(h, wgu_ref[:, FFN_HIDDEN + c0:FFN_HIDDEN + c0 + FFN_CHUNK], preferred_element_type=f32)
        act = (_silu(a) * b).astype(bf16)
        acc = acc + jnp.dot(act, wd_ref[c0:c0 + FFN_CHUNK, :], preferred_element_type=f32)
    o_ref[...] = x + gate_ref[row:row + 1, :] * acc


def _ffn(x, mods, layer, row, gain, w_gu, w_down, tm):
    t = x.shape[0]
    tok = pl.BlockSpec((tm, D_MODEL), lambda i: (i, 0))
    return pl.pallas_call(
        functools.partial(_ffn_kernel, row),
        out_shape=jax.ShapeDtypeStruct((t, D_MODEL), f32),
        grid=(t // tm,),
        in_specs=[tok, _resident((1, D_MODEL)),
                  _mod_spec(layer, SHIFT_F), _mod_spec(layer, SCALE_F), _mod_spec(layer, GATE_F),
                  _resident((D_MODEL, 2 * FFN_HIDDEN)), _resident((FFN_HIDDEN, D_MODEL))],
        out_specs=tok,
        compiler_params=_params("arbitrary"),
        name="ffn",
    )(x, gain, mods, mods, mods, w_gu, w_down)


def _rope_tables(t, quarter, layout):
    n_rows = t // GRID_W
    inv = ROPE_BASE ** (-jnp.arange(quarter, dtype=f32) / quarter)
    ang = {'r': jnp.arange(n_rows).astype(f32)[:, None] * inv, 'c': jnp.arange(GRID_W).astype(f32)[:, None] * inv}
    n = {'r': n_rows, 'c': GRID_W}

    def table(axis, fn, signed):
        parts = []
        for grp in layout:
            if isinstance(grp, int):
                parts.append(jnp.zeros((n[axis], grp), f32))
            elif grp[0] == axis:
                sign = -1.0 if (signed and grp[1] == '1') else 1.0
                parts.append(sign * fn(ang[axis]))
            else:
                parts.append(jnp.zeros((n[axis], quarter), f32))
        return jnp.concatenate(parts, axis=1)

    return (table('r', jnp.cos, False), table('c', jnp.cos, False),
            table('r', jnp.sin, True), table('c', jnp.sin, True))


RET_ROPE_LAYOUT = ('r1', 'r2', 'c1', 'c2')
ATT_ROPE_LAYOUT = ('r1', 'c1', LANES // 2 - ATT_HEAD_DIM // 2, 'r2', 'c2', LANES // 2 - ATT_HEAD_DIM // 2)


def _identity_tables(t, width):
    n_rows = t // GRID_W
    return (jnp.ones((n_rows, width), f32), jnp.zeros((GRID_W, width), f32),
            jnp.zeros((n_rows, width), f32), jnp.zeros((GRID_W, width), f32))


def _pad_heads(w, n_heads):
    lead = w.shape[:-1]
    w = w.reshape(lead + (n_heads, ATT_HEAD_DIM))
    w = jnp.pad(w, [(0, 0)] * len(lead) + [(0, 0), (0, LANES - ATT_HEAD_DIM)])
    return w.reshape(lead + (n_heads * LANES,))


def _pad_heads_rope(w, n_heads):
    lead = w.shape[:-1]
    q4 = ATT_HEAD_DIM // 4
    w = w.reshape(lead + (n_heads, ATT_HEAD_DIM))
    z = jnp.zeros(lead + (n_heads, LANES // 2 - 2 * q4), w.dtype)
    w = jnp.concatenate([w[..., 0:q4], w[..., 2 * q4:3 * q4], z, w[..., q4:2 * q4], w[..., 3 * q4:], z], axis=-1)
    return w.reshape(lead + (n_heads * LANES,))


def kernel(x, c, ctx, c_ctx, ada_w, ada_b, norm_mix, norm_ffn, conv_w1, conv_b1, conv_dw, conv_dw_b, conv_norm, conv_w2, ret_w_in, ret_decay_f, ret_decay_b, ret_w_out, att_w_qkv, att_q_norm, att_k_norm, att_sink, att_w_o, ffn_w_gu, ffn_w_down):
    assert x.shape[0] == 1 and c.shape[0] == 1 and ctx.shape[0] == 1
    t_lat, t_ctx = x.shape[1], ctx.shape[1]
    tm_lat, tm_ctx = 512, t_ctx
    xs, hc = x[0], ctx[0]
    lat, cx = 0, 1

    cond8 = jnp.zeros((8, D_MODEL), f32).at[lat].set(c[0]).at[cx].set(c_ctx)
    mods = _adaln(cond8, ada_w, ada_b)

    row1 = lambda v: v.reshape(1, -1)
    for i in range(DEPTH):
        kind, j, last = i % N_MIXERS, i // N_MIXERS, i == DEPTH - 1
        with_ctx = not last
        g_mix = row1(norm_mix[i])
        if kind == 0:
            w1, w2 = conv_w1[j].astype(bf16), conv_w2[j].astype(bf16)
            cargs = (conv_dw[j], row1(conv_dw_b[j]), row1(conv_norm[j]), w2)
            u = _conv_in(xs, mods, i, lat, g_mix, w1, row1(conv_b1[j]), tm_lat)
            xs = _conv_out(u, xs, mods, i, lat, *cargs, tm_lat)
            if with_ctx:
                u = _conv_in(hc, mods, i, cx, g_mix, w1, row1(conv_b1[j]), tm_ctx)
                hc = _conv_out(u, hc, mods, i, cx, *cargs, tm_ctx)
        elif kind == 1:
            w_in, w_out = ret_w_in[j].astype(bf16), ret_w_out[j].astype(bf16)
            bdec = lambda d: jnp.broadcast_to(d[:, None, None], (RET_HEADS, 8, RET_V_DIM)).astype(f32)
            dec_f, dec_b = bdec(ret_decay_f[j]), bdec(ret_decay_b[j])
            zeros = jnp.zeros((RET_HEADS, RET_QK_DIM, RET_V_DIM), f32)
            qc, kc, vc, gc = _ret_in(hc, mods, i, cx, g_mix, w_in, _identity_tables(t_ctx, RET_QK_DIM), tm_ctx)
            sf_c, sb_c, s0f, s0b = _ret_state(kc, vc, dec_f, dec_b, zeros, zeros)
            qx, kx, vx, gx = _ret_in(xs, mods, i, lat, g_mix, w_in, _rope_tables(t_lat, RET_QK_DIM // 4, RET_ROPE_LAYOUT), tm_lat)
            sf_x, sb_x, _, _ = _ret_state(kx, vx, dec_f, dec_b, s0f, s0b)
            xs = _ret_out(qx, kx, vx, gx, sf_x, sb_x, dec_f, dec_b, w_out, xs, mods, i, lat)
            if with_ctx:
                hc = _ret_out(qc, kc, vc, gc, sf_c, sb_c, dec_f, dec_b, w_out, hc, mods, i, cx)
        else:
            w_qkv = jnp.concatenate([
                _pad_heads_rope(att_w_qkv[j][:, :(ATT_Q_HEADS + ATT_KV_HEADS) * ATT_HEAD_DIM], ATT_Q_HEADS + ATT_KV_HEADS),
                _pad_heads(att_w_qkv[j][:, (ATT_Q_HEADS + ATT_KV_HEADS) * ATT_HEAD_DIM:], ATT_KV_HEADS)], axis=1).astype(bf16)
            w_o = _pad_heads(att_w_o[j].T, ATT_Q_HEADS).T.astype(bf16)
            qg, kg = _pad_heads_rope(row1(att_q_norm[j]), 1), _pad_heads_rope(row1(att_k_norm[j]), 1)
            sink = att_sink[j].astype(f32)
            qc, kc, vc = _att_in(hc, mods, i, cx, g_mix, w_qkv, qg, kg, _identity_tables(t_ctx, LANES), tm_ctx)
            qx, kx, vx = _att_in(xs, mods, i, lat, g_mix, w_qkv, qg, kg, _rope_tables(t_lat, ATT_HEAD_DIM // 4, ATT_ROPE_LAYOUT), tm_lat)
            xs = _attention(sink, qx, kc, vc, kx, vx, w_o, xs, mods, i, lat)
            if with_ctx:
                hc = _attention(sink, qc, kc, vc, None, None, w_o, hc, mods, i, cx)
        g_ffn = row1(norm_ffn[i])
        w_gu, w_down = ffn_w_gu[i].astype(bf16), ffn_w_down[i].astype(bf16)
        xs = _ffn(xs, mods, i, lat, g_ffn, w_gu, w_down, tm_lat)
        if with_ctx:
            hc = _ffn(hc, mods, i, cx, g_ffn, w_gu, w_down, tm_ctx)
    return xs[None]
```

```python
import functools

import jax
import jax.numpy as jnp
from jax import lax
from jax.experimental import pallas as pl
from jax.experimental.pallas import tpu as pltpu

f32 = jnp.float32
bf16 = jnp.bfloat16

D_MODEL = 1024
DEPTH = 4
GRID_W = 64
N_MIXERS = 3
CONV_WIDTH = 31
CONV_HALO = 16
CONV_ROWS = 128
RET_HEADS = 4
RET_QK_DIM = 256
RET_V_DIM = 512
RET_CHUNK = 256
ATT_Q_HEADS = 16
ATT_KV_HEADS = 4
ATT_GROUP = ATT_Q_HEADS // ATT_KV_HEADS
ATT_HEAD_DIM = 64
ATT_WINDOW = 128
ATT_BLOCK = 128
ATT_ROWS = 32
FFN_HIDDEN = 2816
FFN_CHUNK = 256
CAST_GU_STEPS = 32
CAST_DN_STEPS = 16
ROPE_BASE = 10000.0
NORM_EPS = 1e-6
NEG_INF = -1e30
LOG2_E = 1.4426950408889634
LANES = 128
VMEM_LIMIT = 56 * 1024 * 1024

SHIFT_M, SCALE_M, GATE_M, SHIFT_F, SCALE_F, GATE_F = range(6)


def _params(*sem):
    return pltpu.CompilerParams(dimension_semantics=sem, vmem_limit_bytes=VMEM_LIMIT)


def _resident(shape):
    nd = len(shape)
    return pl.BlockSpec(shape, lambda *_: (0,) * nd, pipeline_mode=pl.Buffered(1))


def _mod_spec(layer, which):
    return pl.BlockSpec((None, 8, D_MODEL), lambda *_: (layer, 0, which))


def _modulated(x, gain, scale, shift):
    ms = jnp.mean(x * x, axis=-1, keepdims=True)
    return (x * lax.rsqrt(ms + NORM_EPS)) * (gain * (1.0 + scale)) + shift


def _silu(v):
    return v * jax.nn.sigmoid(v)


def _tile_table(row_ref, col_ref):
    return jnp.concatenate([row_ref[r:r + 1, :] + col_ref[...] for r in range(row_ref.shape[0])], axis=0)


def _table_specs(tm, width):
    row = pl.BlockSpec((tm // GRID_W, width), lambda i: (i, 0))
    col = _resident((GRID_W, width))
    return [row, col, row, col]


def _cast_io(layer):
    gu_rows = D_MODEL // CAST_GU_STEPS
    dn_rows = FFN_HIDDEN // CAST_DN_STEPS
    gu_i = lambda i: jnp.minimum(i, CAST_GU_STEPS - 1)
    dn_i = lambda i: jnp.minimum(i, CAST_DN_STEPS - 1)
    in_specs = [pl.BlockSpec((None, gu_rows, 2 * FFN_HIDDEN), lambda i: (layer, gu_i(i), 0)),
                pl.BlockSpec((None, dn_rows, D_MODEL), lambda i: (layer, dn_i(i), 0))]
    out_specs = [pl.BlockSpec((gu_rows, 2 * FFN_HIDDEN), lambda i: (gu_i(i), 0)),
                 pl.BlockSpec((dn_rows, D_MODEL), lambda i: (dn_i(i), 0))]
    out_shape = [jax.ShapeDtypeStruct((D_MODEL, 2 * FFN_HIDDEN), bf16),
                 jax.ShapeDtypeStruct((FFN_HIDDEN, D_MODEL), bf16)]
    return in_specs, out_specs, out_shape


def _cast_weights(cast_refs):
    if not cast_refs:
        return
    gu_ref, dn_ref, gu_out_ref, dn_out_ref = cast_refs
    i = pl.program_id(0)

    @pl.when(i < CAST_GU_STEPS)
    def _():
        gu_out_ref[...] = gu_ref[...].astype(bf16)

    @pl.when(i < CAST_DN_STEPS)
    def _():
        dn_out_ref[...] = dn_ref[...].astype(bf16)


def _split_refs(refs, n_in, n_out, cast):
    extra = 2 if cast else 0
    ins = refs[:n_in]
    outs = refs[n_in + extra:n_in + extra + n_out]
    scratch = refs[n_in + 2 * extra + n_out:]
    cast_refs = refs[n_in:n_in + extra] + refs[n_in + extra + n_out:n_in + 2 * extra + n_out]
    return ins, outs, scratch, cast_refs


def _adaln_kernel(c_ref, w_ref, b_ref, o_ref):
    s = _silu(c_ref[...])
    o_ref[...] = jnp.dot(s, w_ref[...], precision=lax.Precision.HIGHEST,
                         preferred_element_type=f32) + b_ref[...]


def _adaln(cond8, ada_w, ada_b):
    tn = 1536
    n = 6 * D_MODEL
    return pl.pallas_call(
        _adaln_kernel,
        out_shape=jax.ShapeDtypeStruct((DEPTH, 8, n), f32),
        grid=(DEPTH, n // tn),
        in_specs=[
            pl.BlockSpec((8, D_MODEL), lambda l, j: (0, 0)),
            pl.BlockSpec((None, D_MODEL, tn), lambda l, j: (l, 0, j)),
            pl.BlockSpec((None, 1, tn), lambda l, j: (l, 0, j)),
        ],
        out_specs=pl.BlockSpec((None, 8, tn), lambda l, j: (l, 0, j)),
        compiler_params=_params("arbitrary", "arbitrary"),
        name="adaln",
    )(cond8, ada_w, ada_b.reshape(DEPTH, 1, n))


def _conv_kernel(row, n_tiles, cast, *refs):
    ins, (o_ref,), (xw_ref, win_ref, acc_ref, y_ref), cast_refs = _split_refs(refs, 13, 1, cast)
    xm_ref, xp_ref, xn_ref, g_ref, sh_ref, sc_ref, w1_ref, b1_ref, dw_ref, dwb_ref, ng_ref, w2_ref, gate_ref = ins
    _cast_weights(cast_refs)
    i = pl.program_id(0)
    tm = xm_ref.shape[0]
    rb = CONV_ROWS
    xw_ref[0:CONV_HALO, :] = xp_ref[...]
    xw_ref[CONV_HALO:CONV_HALO + tm, :] = xm_ref[...]
    xw_ref[CONV_HALO + tm:, :] = xn_ref[...]
    h = _modulated(xw_ref[...], g_ref[...], sc_ref[row:row + 1, :], sh_ref[row:row + 1, :])
    y = jnp.dot(h.astype(bf16), w1_ref[...], preferred_element_type=f32) + b1_ref[...]
    win_ref[...] = y[:, :D_MODEL] * jax.nn.sigmoid(y[:, D_MODEL:])

    @pl.when(i == 0)
    def _():
        win_ref[0:CONV_HALO, :] = jnp.zeros((CONV_HALO, D_MODEL), f32)

    @pl.when(i == n_tiles - 1)
    def _():
        win_ref[CONV_HALO + tm:, :] = jnp.zeros((CONV_HALO, D_MODEL), f32)

    off = CONV_HALO - CONV_WIDTH // 2

    def body(r, carry):
        base = pl.multiple_of(r * rb, rb)
        for lb in range(D_MODEL // LANES):
            ls = slice(lb * LANES, (lb + 1) * LANES)
            win = win_ref[pl.ds(base, rb + 2 * CONV_HALO), ls]
            out = jnp.broadcast_to(dwb_ref[:, ls], (rb, LANES))
            for s in range(8):
                z = None
                for k in range(CONV_WIDTH):
                    if (off + k) % 8 != s:
                        continue
                    j = (off + k) // 8
                    term = win[8 * j:8 * j + rb + 8, :] * dw_ref[k:k + 1, ls]
                    z = term if z is None else z + term
                if z is None:
                    continue
                out = out + (z[0:rb] if s == 0 else pltpu.roll(z, rb + 8 - s, axis=0)[0:rb])
            acc_ref[:, ls] = out
        acc = acc_ref[...]
        mu = jnp.mean(acc, axis=-1, keepdims=True)
        xc = acc - mu
        var = jnp.mean(xc * xc, axis=-1, keepdims=True)
        yn = xc * lax.rsqrt(var + NORM_EPS) * ng_ref[...]
        y_ref[pl.ds(base, rb), :] = _silu(yn).astype(bf16)
        return carry

    lax.fori_loop(0, tm // rb, body, 0)
    o = jnp.dot(y_ref[...], w2_ref[...], preferred_element_type=f32)
    o_ref[...] = xm_ref[...] + gate_ref[row:row + 1, :] * o


def _conv_mixer(x, mods, layer, row, gain, w1, b1, dw, dw_b, norm_g, w2, tm, ffn_w=None):
    t = x.shape[0]
    cast = ffn_w is not None
    c_in, c_out, c_shape = _cast_io(layer) if cast else ([], [], [])
    n_tiles = t // tm
    hb = tm // CONV_HALO
    n_hb = t // CONV_HALO
    return pl.pallas_call(
        functools.partial(_conv_kernel, row, n_tiles, cast),
        out_shape=[jax.ShapeDtypeStruct((t, D_MODEL), f32)] + c_shape,
        grid=(n_tiles,),
        in_specs=[
            pl.BlockSpec((tm, D_MODEL), lambda i: (i, 0)),
            pl.BlockSpec((CONV_HALO, D_MODEL), lambda i: (jnp.maximum(i * hb - 1, 0), 0)),
            pl.BlockSpec((CONV_HALO, D_MODEL), lambda i: (jnp.minimum((i + 1) * hb, n_hb - 1), 0)),
            _resident((1, D_MODEL)),
            _mod_spec(layer, SHIFT_M),
            _mod_spec(layer, SCALE_M),
            _resident((D_MODEL, 2 * D_MODEL)),
            _resident((1, 2 * D_MODEL)),
            _resident((CONV_WIDTH, D_MODEL)),
            _resident((1, D_MODEL)),
            _resident((1, D_MODEL)),
            _resident((D_MODEL, D_MODEL)),
            _mod_spec(layer, GATE_M),
        ] + c_in,
        out_specs=[pl.BlockSpec((tm, D_MODEL), lambda i: (i, 0))] + c_out,
        scratch_shapes=[
            pltpu.VMEM((tm + 2 * CONV_HALO, D_MODEL), f32),
            pltpu.VMEM((tm + 2 * CONV_HALO, D_MODEL), f32),
            pltpu.VMEM((CONV_ROWS, D_MODEL), f32),
            pltpu.VMEM((tm, D_MODEL), bf16),
        ],
        compiler_params=_params("arbitrary"),
        name="conv_mixer",
    )(x, x, x, gain, mods, mods, w1, b1, dw, dw_b, norm_g, w2, mods, *(ffn_w or ()))


def _ret_in_kernel(row, x_ref, g_ref, sh_ref, sc_ref, w_ref, rcos_ref, ccos_ref, rsin_ref, csin_ref,
                   q_ref, k_ref, v_ref, sg_ref):
    h = _modulated(x_ref[...], g_ref[...], sc_ref[row:row + 1, :], sh_ref[row:row + 1, :]).astype(bf16)
    hk = RET_HEADS * RET_QK_DIM
    hv = RET_HEADS * RET_V_DIM
    cos_t = _tile_table(rcos_ref, ccos_ref)
    sin_t = _tile_table(rsin_ref, csin_ref)

    def rope(y, scale):
        outs = []
        for b in range(hk // LANES):
            yb = y[:, b * LANES:(b + 1) * LANES]
            tb = (b % 2) * LANES
            rot = yb * cos_t[:, tb:tb + LANES] + pltpu.roll(yb, LANES // 2, axis=1) * sin_t[:, tb:tb + LANES]
            outs.append(rot * scale)
        return jnp.concatenate(outs, axis=1)

    q = jnp.dot(h, w_ref[:, 0:hk], preferred_element_type=f32)
    q_ref[...] = rope(q, 1.0).astype(bf16)
    k = jnp.dot(h, w_ref[:, hk:2 * hk], preferred_element_type=f32)
    k_ref[...] = rope(k, RET_QK_DIM ** -0.5).astype(bf16)
    v_ref[...] = jnp.dot(h, w_ref[:, 2 * hk:2 * hk + hv], preferred_element_type=f32).astype(bf16)
    g = jnp.dot(h, w_ref[:, 2 * hk + hv:], preferred_element_type=f32)
    sg_ref[...] = _silu(g).astype(bf16)


def _ret_in(x, mods, layer, row, gain, w_in, tabs, tm):
    t = x.shape[0]
    hk = RET_HEADS * RET_QK_DIM
    hv = RET_HEADS * RET_V_DIM
    tok = lambda n: pl.BlockSpec((tm, n), lambda i: (i, 0))
    return pl.pallas_call(
        functools.partial(_ret_in_kernel, row),
        out_shape=[jax.ShapeDtypeStruct((t, hk), bf16), jax.ShapeDtypeStruct((t, hk), bf16),
                   jax.ShapeDtypeStruct((t, hv), bf16), jax.ShapeDtypeStruct((t, hv), bf16)],
        grid=(t // tm,),
        in_specs=[
            tok(D_MODEL),
            _resident((1, D_MODEL)),
            _mod_spec(layer, SHIFT_M),
            _mod_spec(layer, SCALE_M),
            _resident((D_MODEL, 2 * hk + 2 * hv)),
            *_table_specs(tm, RET_QK_DIM),
        ],
        out_specs=[tok(hk), tok(hk), tok(hv), tok(hv)],
        compiler_params=_params("arbitrary"),
        name="ret_in",
    )(x, gain, mods, mods, w_in, *tabs)


def _log_sigmoid(v):
    return jnp.minimum(v, 0.0) - jnp.log1p(jnp.exp(-jnp.abs(v)))


def _ret_state_kernel(k_ref, v_ref, decb_ref, s0b_ref, sball_ref, sb_ref):
    c = RET_CHUNK

    @pl.when(pl.program_id(0) == 0)
    def _():
        sb_ref[...] = s0b_ref[...]

    idx = lax.broadcasted_iota(jnp.int32, (c, 1), 0).astype(f32)
    for h in range(RET_HEADS):
        lg_b = _log_sigmoid(decb_ref[h, 0:1, :])
        kd = jnp.exp(lg_b[:, 0:1] * idx)
        kh = (k_ref[:, h * RET_QK_DIM:(h + 1) * RET_QK_DIM].astype(f32) * kd).astype(bf16)
        a = lax.dot_general(kh, v_ref[:, h * RET_V_DIM:(h + 1) * RET_V_DIM], (((0,), (0,)), ((), ())),
                            preferred_element_type=f32)
        s = sb_ref[h]
        sball_ref[h] = s.astype(bf16)
        sb_ref[h] = s * jnp.exp(lg_b * c) + a


def _ret_state(k, v, dec_b, s0b):
    t = k.shape[0]
    n = t // RET_CHUNK
    st = (RET_HEADS, RET_QK_DIM, RET_V_DIM)
    bwd = lambda w: pl.BlockSpec((RET_CHUNK, w), lambda i: (n - 1 - i, 0))
    return pl.pallas_call(
        _ret_state_kernel,
        out_shape=[jax.ShapeDtypeStruct((n,) + st, bf16), jax.ShapeDtypeStruct(st, f32)],
        grid=(n,),
        in_specs=[bwd(RET_HEADS * RET_QK_DIM), bwd(RET_HEADS * RET_V_DIM),
                  _resident((RET_HEADS, 8, RET_V_DIM)), _resident(st)],
        out_specs=[pl.BlockSpec((None,) + st, lambda i: (n - 1 - i, 0, 0, 0)),
                   pl.BlockSpec(st, lambda i: (0, 0, 0))],
        compiler_params=_params("arbitrary"),
        name="ret_state",
    )(k, v, dec_b, s0b)


def _ret_out_kernel(row, cast, *refs):
    ins, (o_ref, sf_ref), (dec_ref, y_ref), cast_refs = _split_refs(refs, 11, 2, cast)
    q_ref, k_ref, v_ref, sg_ref, sb_ref, s0f_ref, decf_ref, decb_ref, w_ref, x_ref, gate_ref = ins
    _cast_weights(cast_refs)
    c = RET_CHUNK

    @pl.when(pl.program_id(0) == 0)
    def _():
        sf_ref[...] = s0f_ref[...]
        t_i = lax.broadcasted_iota(jnp.int32, (c, c), 0)
        m_i = lax.broadcasted_iota(jnp.int32, (c, c), 1)
        rel = (t_i - m_i).astype(f32)
        for h in range(RET_HEADS):
            lg_f = _log_sigmoid(decf_ref[h, 0:1, :c])
            lg_b = _log_sigmoid(decb_ref[h, 0:1, :c])
            d_f = jnp.where(rel >= 0, jnp.exp(lg_f * jnp.maximum(rel, 0.0)), 0.0)
            d_b = jnp.where(rel <= 0, jnp.exp(lg_b * jnp.maximum(-rel, 0.0)), 0.0)
            dec_ref[h] = d_f + d_b

    idx = lax.broadcasted_iota(jnp.int32, (c, 1), 0).astype(f32)
    for h in range(RET_HEADS):
        lg_f = _log_sigmoid(decf_ref[h, 0:1, :])
        lg_b = _log_sigmoid(decb_ref[h, 0:1, :])
        q = q_ref[:, h * RET_QK_DIM:(h + 1) * RET_QK_DIM]
        k = k_ref[:, h * RET_QK_DIM:(h + 1) * RET_QK_DIM]
        v = v_ref[:, h * RET_V_DIM:(h + 1) * RET_V_DIM]
        s_f = sf_ref[h]
        s = lax.dot_general(q, k, (((1,), (1,)), ((), ())), preferred_element_type=f32)
        o = jnp.dot((s * dec_ref[h]).astype(bf16), v, preferred_element_type=f32)
        o = o + jnp.exp(lg_f[:, 0:1] * (idx + 1.0)) * jnp.dot(q, s_f.astype(bf16), preferred_element_type=f32)
        o = o + jnp.exp(lg_b[:, 0:1] * (c - idx)) * jnp.dot(q, sb_ref[h], preferred_element_type=f32)
        o = o * lax.rsqrt(jnp.mean(o * o, axis=-1, keepdims=True) + NORM_EPS)
        sg = sg_ref[:, h * RET_V_DIM:(h + 1) * RET_V_DIM].astype(f32)
        y_ref[:, h * RET_V_DIM:(h + 1) * RET_V_DIM] = (sg * o).astype(bf16)
        kd = (k.astype(f32) * jnp.exp(lg_f[:, 0:1] * (c - 1.0 - idx))).astype(bf16)
        a = lax.dot_general(kd, v, (((0,), (0,)), ((), ())), preferred_element_type=f32)
        sf_ref[h] = s_f * jnp.exp(lg_f * c) + a
    out = jnp.dot(y_ref[...], w_ref[...], preferred_element_type=f32)
    o_ref[...] = x_ref[...] + gate_ref[row:row + 1, :] * out


def _ret_out(q, k, v, sg, sb_all, s0f, dec_f, dec_b, w_out, x, mods, layer, row, ffn_w=None):
    t = x.shape[0]
    cast = ffn_w is not None
    c_in, c_out, c_shape = _cast_io(layer) if cast else ([], [], [])
    n = t // RET_CHUNK
    hk = RET_HEADS * RET_QK_DIM
    hv = RET_HEADS * RET_V_DIM
    st = (RET_HEADS, RET_QK_DIM, RET_V_DIM)
    tok = lambda w: pl.BlockSpec((RET_CHUNK, w), lambda i: (i, 0))
    return pl.pallas_call(
        functools.partial(_ret_out_kernel, row, cast),
        out_shape=[jax.ShapeDtypeStruct((t, D_MODEL), f32), jax.ShapeDtypeStruct(st, f32)] + c_shape,
        grid=(n,),
        in_specs=[tok(hk), tok(hk), tok(hv), tok(hv),
                  pl.BlockSpec((None,) + st, lambda i: (i, 0, 0, 0)),
                  _resident(st),
                  _resident((RET_HEADS, 8, RET_V_DIM)), _resident((RET_HEADS, 8, RET_V_DIM)),
                  _resident((hv, D_MODEL)),
                  tok(D_MODEL),
                  _mod_spec(layer, GATE_M)] + c_in,
        out_specs=[tok(D_MODEL), pl.BlockSpec(st, lambda i: (0, 0, 0))] + c_out,
        scratch_shapes=[pltpu.VMEM((RET_HEADS, RET_CHUNK, RET_CHUNK), f32),
                        pltpu.VMEM((RET_CHUNK, hv), bf16)],
        compiler_params=_params("arbitrary"),
        name="ret_out",
    )(q, k, v, sg, sb_all, s0f, dec_f, dec_b, w_out, x, mods, *(ffn_w or ()))


def _att_in_kernel(row, x_ref, g_ref, sh_ref, sc_ref, w_ref, qg_ref, kg_ref,
                   rcos_ref, ccos_ref, rsin_ref, csin_ref, q_ref, k_ref, v_ref):
    h = _modulated(x_ref[...], g_ref[...], sc_ref[row:row + 1, :], sh_ref[row:row + 1, :]).astype(bf16)
    nq = ATT_Q_HEADS * LANES
    nk = ATT_KV_HEADS * LANES
    cos_t = _tile_table(rcos_ref, ccos_ref)
    sin_t = _tile_table(rsin_ref, csin_ref)

    def partner(t):
        return pltpu.roll(t, LANES // 2, axis=1)

    def tables(gain_ref, scale):
        gain = jnp.broadcast_to(gain_ref[...], (8, LANES))
        return cos_t * (gain[0:1, :] * scale), sin_t * (partner(gain)[0:1, :] * scale)

    def head_norm_rope(y, cos_g, sin_g):
        ms = jnp.sum(y * y, axis=-1, keepdims=True) * (1.0 / ATT_HEAD_DIM)
        return (y * cos_g + partner(y) * sin_g) * lax.rsqrt(ms + NORM_EPS)

    q_tabs = tables(qg_ref, ATT_HEAD_DIM ** -0.5 * LOG2_E)
    k_tabs = tables(kg_ref, 1.0)
    for c0 in range(0, nq + nk, 2 * LANES):
        y = jnp.dot(h, w_ref[:, c0:c0 + 2 * LANES], preferred_element_type=f32)
        for col in (c0, c0 + LANES):
            yh = y[:, col - c0:col - c0 + LANES]
            if col < nq:
                q_ref[:, col:col + LANES] = head_norm_rope(yh, *q_tabs).astype(bf16)
            else:
                k_ref[:, col - nq:col - nq + LANES] = head_norm_rope(yh, *k_tabs).astype(bf16)
    v_ref[...] = jnp.dot(h, w_ref[:, nq + nk:], preferred_element_type=f32).astype(bf16)


def _att_in(x, mods, layer, row, gain, w_qkv_p, q_gain_p, k_gain_p, tabs, tm):
    t = x.shape[0]
    nq = ATT_Q_HEADS * LANES
    nk = ATT_KV_HEADS * LANES
    tok = lambda n: pl.BlockSpec((tm, n), lambda i: (i, 0))
    return pl.pallas_call(
        functools.partial(_att_in_kernel, row),
        out_shape=[jax.ShapeDtypeStruct((t, nq), bf16), jax.ShapeDtypeStruct((t, nk), bf16),
                   jax.ShapeDtypeStruct((t, nk), bf16)],
        grid=(t // tm,),
        in_specs=[
            tok(D_MODEL),
            _resident((1, D_MODEL)),
            _mod_spec(layer, SHIFT_M),
            _mod_spec(layer, SCALE_M),
            _resident((D_MODEL, nq + 2 * nk)),
            _resident((1, LANES)),
            _resident((1, LANES)),
            *_table_specs(tm, LANES),
        ],
        out_specs=[tok(nq), tok(nk), tok(nk)],
        compiler_params=_params("arbitrary"),
        name="att_in",
    )(x, gain, mods, mods, w_qkv_p, q_gain_p, k_gain_p, *tabs)


def _att_kernel(row, n_blocks, band, cast, *refs):
    ins, (o_ref,), (s_ref, e_ref, l_ref, y_ref), cast_refs = _split_refs(refs, 13 if band else 7, 1, cast)
    if band:
        sink_ref, q_ref, kc_ref, vc_ref, kp_ref, kn_ref, kx_ref, vp_ref, vn_ref, vx_ref, w_ref, x_ref, gate_ref = ins
    else:
        sink_ref, q_ref, kc_ref, vc_ref, w_ref, x_ref, gate_ref = ins
    _cast_weights(cast_refs)
    i = pl.program_id(0)
    c = ATT_BLOCK
    g = ATT_GROUP
    rc = ATT_ROWS
    n_ctx = kc_ref.shape[0]
    if band:
        a = lax.broadcasted_iota(jnp.int32, (c, c), 0)
        j = lax.broadcasted_iota(jnp.int32, (c, c), 1)
        prev_ok = (j >= a + (c - ATT_WINDOW)) & (i > 0)
        next_ok = (j <= a + (ATT_WINDOW - c)) & (i < n_blocks - 1)
    for kh in range(ATT_KV_HEADS):
        sl = slice(kh * LANES, (kh + 1) * LANES)
        q = jnp.concatenate([q_ref[:, (kh * g + gg) * LANES:(kh * g + gg + 1) * LANES] for gg in range(g)], axis=0)
        if band:
            keys = jnp.concatenate([kc_ref[:, sl], kp_ref[:, sl], kx_ref[:, sl], kn_ref[:, sl]], axis=0)
            vals = jnp.concatenate([vc_ref[:, sl], vp_ref[:, sl], vx_ref[:, sl], vn_ref[:, sl]], axis=0)
        else:
            keys = kc_ref[:, sl]
            vals = vc_ref[:, sl]
        s_ref[kh] = lax.dot_general(q, keys, (((1,), (1,)), ((), ())), preferred_element_type=f32)
        for gg in range(g):
            sink = sink_ref[kh * g + gg] * LOG2_E
            for r0 in range(0, c, rc):
                rows = slice(gg * c + r0, gg * c + r0 + rc)
                s = s_ref[kh, rows, :]
                if band:
                    s = jnp.concatenate([
                        s[:, :n_ctx],
                        jnp.where(prev_ok[r0:r0 + rc, :], s[:, n_ctx:n_ctx + c], NEG_INF),
                        s[:, n_ctx + c:n_ctx + 2 * c],
                        jnp.where(next_ok[r0:r0 + rc, :], s[:, n_ctx + 2 * c:], NEG_INF)], axis=1)
                m = jnp.maximum(jnp.max(s, axis=-1, keepdims=True), sink)
                e = jnp.exp2(s - m)
                e_ref[kh, rows, :] = e.astype(bf16)
                l_ref[kh, rows, :] = 1.0 / (jnp.sum(e, axis=-1, keepdims=True) + jnp.exp2(sink - m))
        o = jnp.dot(e_ref[kh], vals, preferred_element_type=f32) * l_ref[kh]
        for gg in range(g):
            hd = kh * g + gg
            y_ref[:, hd * LANES:(hd + 1) * LANES] = o[gg * c:(gg + 1) * c, :].astype(bf16)
    out = jnp.dot(y_ref[...], w_ref[...], preferred_element_type=f32)
    o_ref[...] = x_ref[...] + gate_ref[row:row + 1, :] * out


def _attention(sink, q, kc, vc, kx, vx, w_o_p, x, mods, layer, row, ffn_w=None):
    t = x.shape[0]
    cast = ffn_w is not None
    c_in, c_out, c_shape = _cast_io(layer) if cast else ([], [], [])
    c = ATT_BLOCK
    n = t // c
    nq = ATT_Q_HEADS * LANES
    nk = ATT_KV_HEADS * LANES
    band = kx is not None
    n_keys = kc.shape[0] + (3 * c if band else 0)
    tok = lambda w: pl.BlockSpec((c, w), lambda i: (i, 0))
    prev = pl.BlockSpec((c, nk), lambda i: (jnp.maximum(i - 1, 0), 0))
    nxt = pl.BlockSpec((c, nk), lambda i: (jnp.minimum(i + 1, n - 1), 0))
    in_specs = [pl.BlockSpec(memory_space=pltpu.SMEM), tok(nq),
                _resident(kc.shape), _resident(vc.shape)]
    args = [sink, q, kc, vc]
    if band:
        in_specs += [prev, nxt, tok(nk), prev, nxt, tok(nk)]
        args += [kx, kx, kx, vx, vx, vx]
    in_specs += [_resident((nq, D_MODEL)), tok(D_MODEL), _mod_spec(layer, GATE_M)] + c_in
    args += [w_o_p, x, mods, *(ffn_w or ())]
    return pl.pallas_call(
        functools.partial(_att_kernel, row, n, band, cast),
        out_shape=[jax.ShapeDtypeStruct((t, D_MODEL), f32)] + c_shape,
        grid=(n,),
        in_specs=in_specs,
        out_specs=[tok(D_MODEL)] + c_out,
        scratch_shapes=[pltpu.VMEM((ATT_KV_HEADS, ATT_GROUP * c, n_keys), f32),
                        pltpu.VMEM((ATT_KV_HEADS, ATT_GROUP * c, n_keys), bf16),
                        pltpu.VMEM((ATT_KV_HEADS, ATT_GROUP * c, 1), f32),
                        pltpu.VMEM((c, nq), bf16)],
        compiler_params=_params("arbitrary"),
        name="att_band" if band else "att_ctx",
    )(*args)


def _ffn_kernel(row, x_ref, g_ref, sh_ref, sc_ref, gate_ref, wgu_ref, wd_ref, o_ref):
    x = x_ref[...]
    h = _modulated(x, g_ref[...], sc_ref[row:row + 1, :], sh_ref[row:row + 1, :]).astype(bf16)
    acc = jnp.zeros(x.shape, f32)
    for c0 in range(0, FFN_HIDDEN, FFN_CHUNK):
        a = jnp.dot(h, wgu_ref[:, c0:c0 + FFN_CHUNK], preferred_element_type=f32)
        b = jnp.dot(h, wgu_ref[:, FFN_HIDDEN + c0:FFN_HIDDEN + c0 + FFN_CHUNK], preferred_element_type=f32)
        act = (_silu(a) * b).astype(bf16)
        acc = acc + jnp.dot(act, wd_ref[c0:c0 + FFN_CHUNK, :], preferred_element_type=f32)
    o_ref[...] = x + gate_ref[row:row + 1, :] * acc


def _ffn(x, mods, layer, row, gain, w_gu, w_down, tm):
    t = x.shape[0]
    tok = pl.BlockSpec((tm, D_MODEL), lambda i: (i, 0))
    return pl.pallas_call(
        functools.partial(_ffn_kernel, row),
        out_shape=jax.ShapeDtypeStruct((t, D_MODEL), f32),
        grid=(t // tm,),
        in_specs=[tok, _resident((1, D_MODEL)),
                  _mod_spec(layer, SHIFT_F), _mod_spec(layer, SCALE_F), _mod_spec(layer, GATE_F),
                  _resident((D_MODEL, 2 * FFN_HIDDEN)), _resident((FFN_HIDDEN, D_MODEL))],
        out_specs=tok,
        compiler_params=_params("arbitrary"),
        name="ffn",
    )(x, gain, mods, mods, mods, w_gu, w_down)


def _rope_tables(t, quarter, layout):
    n_rows = t // GRID_W
    inv = ROPE_BASE ** (-jnp.arange(quarter, dtype=f32) / quarter)
    ang = {'r': jnp.arange(n_rows).astype(f32)[:, None] * inv, 'c': jnp.arange(GRID_W).astype(f32)[:, None] * inv}
    n = {'r': n_rows, 'c': GRID_W}

    def table(axis, fn, signed):
        parts = []
        for grp in layout:
            if isinstance(grp, int):
                parts.append(jnp.zeros((n[axis], grp), f32))
            elif grp[0] == axis:
                sign = -1.0 if (signed and grp[1] == '1') else 1.0
                parts.append(sign * fn(ang[axis]))
            else:
                parts.append(jnp.zeros((n[axis], quarter), f32))
        return jnp.concatenate(parts, axis=1)

    return (table('r', jnp.cos, False), table('c', jnp.cos, False),
            table('r', jnp.sin, True), table('c', jnp.sin, True))


RET_ROPE_LAYOUT = ('r1', 'r2', 'c1', 'c2')
ATT_ROPE_LAYOUT = ('r1', 'c1', LANES // 2 - ATT_HEAD_DIM // 2, 'r2', 'c2', LANES // 2 - ATT_HEAD_DIM // 2)


def _identity_tables(t, width):
    n_rows = t // GRID_W
    return (jnp.ones((n_rows, width), f32), jnp.zeros((GRID_W, width), f32),
            jnp.zeros((n_rows, width), f32), jnp.zeros((GRID_W, width), f32))


def _pad_heads(w, n_heads):
    lead = w.shape[:-1]
    w = w.reshape(lead + (n_heads, ATT_HEAD_DIM))
    w = jnp.pad(w, [(0, 0)] * len(lead) + [(0, 0), (0, LANES - ATT_HEAD_DIM)])
    return w.reshape(lead + (n_heads * LANES,))


def _pad_heads_rope(w, n_heads):
    lead = w.shape[:-1]
    q4 = ATT_HEAD_DIM // 4
    w = w.reshape(lead + (n_heads, ATT_HEAD_DIM))
    z = jnp.zeros(lead + (n_heads, LANES // 2 - 2 * q4), w.dtype)
    w = jnp.concatenate([w[..., 0:q4], w[..., 2 * q4:3 * q4], z, w[..., q4:2 * q4], w[..., 3 * q4:], z], axis=-1)
    return w.reshape(lead + (n_heads * LANES,))


def kernel(x, c, ctx, c_ctx, ada_w, ada_b, norm_mix, norm_ffn, conv_w1, conv_b1, conv_dw, conv_dw_b, conv_norm, conv_w2, ret_w_in, ret_decay_f, ret_decay_b, ret_w_out, att_w_qkv, att_q_norm, att_k_norm, att_sink, att_w_o, ffn_w_gu, ffn_w_down):
    assert x.shape[0] == 1 and c.shape[0] == 1 and ctx.shape[0] == 1
    t_lat, t_ctx = x.shape[1], ctx.shape[1]
    tm_lat, tm_ctx = 512, t_ctx
    xs, hc = x[0], ctx[0]
    lat, cx = 0, 1

    cond8 = jnp.zeros((8, D_MODEL), f32).at[lat].set(c[0]).at[cx].set(c_ctx)
    mods = _adaln(cond8, ada_w, ada_b)

    row1 = lambda v: v.reshape(1, -1)
    ffn_w = (ffn_w_gu, ffn_w_down)
    for i in range(DEPTH):
        kind, j, last = i % N_MIXERS, i // N_MIXERS, i == DEPTH - 1
        with_ctx = not last
        g_mix = row1(norm_mix[i])
        if kind == 0:
            w1, w2 = conv_w1[j].astype(bf16), conv_w2[j].astype(bf16)
            cargs = (g_mix, w1, row1(conv_b1[j]), conv_dw[j], row1(conv_dw_b[j]), row1(conv_norm[j]), w2)
            xs, w_gu, w_down = _conv_mixer(xs, mods, i, lat, *cargs, tm_lat, ffn_w=ffn_w)
            if with_ctx:
                hc, = _conv_mixer(hc, mods, i, cx, *cargs, tm_ctx)
        elif kind == 1:
            w_in, w_out = ret_w_in[j].astype(bf16), ret_w_out[j].astype(bf16)
            bdec = lambda d: jnp.broadcast_to(d[:, None, None], (RET_HEADS, 8, RET_V_DIM)).astype(f32)
            dec_f, dec_b = bdec(ret_decay_f[j]), bdec(ret_decay_b[j])
            zeros = jnp.zeros((RET_HEADS, RET_QK_DIM, RET_V_DIM), f32)
            qc, kc, vc, gc = _ret_in(hc, mods, i, cx, g_mix, w_in, _identity_tables(t_ctx, RET_QK_DIM), tm_ctx)
            sb_c, s0b = _ret_state(kc, vc, dec_b, zeros)
            hc_new, s0f = _ret_out(qc, kc, vc, gc, sb_c, zeros, dec_f, dec_b, w_out, hc, mods, i, cx)
            qx, kx, vx, gx = _ret_in(xs, mods, i, lat, g_mix, w_in, _rope_tables(t_lat, RET_QK_DIM // 4, RET_ROPE_LAYOUT), tm_lat)
            sb_x, _ = _ret_state(kx, vx, dec_b, s0b)
            xs, _, w_gu, w_down = _ret_out(qx, kx, vx, gx, sb_x, s0f, dec_f, dec_b, w_out, xs, mods, i, lat, ffn_w=ffn_w)
            if with_ctx:
                hc = hc_new
        else:
            w_qkv = jnp.concatenate([
                _pad_heads_rope(att_w_qkv[j][:, :(ATT_Q_HEADS + ATT_KV_HEADS) * ATT_HEAD_DIM], ATT_Q_HEADS + ATT_KV_HEADS),
                _pad_heads(att_w_qkv[j][:, (ATT_Q_HEADS + ATT_KV_HEADS) * ATT_HEAD_DIM:], ATT_KV_HEADS)], axis=1).astype(bf16)
            w_o = _pad_heads(att_w_o[j].T, ATT_Q_HEADS).T.astype(bf16)
            qg, kg = _pad_heads_rope(row1(att_q_norm[j]), 1), _pad_heads_rope(row1(att_k_norm[j]), 1)
            sink = att_sink[j].astype(f32)
            qc, kc, vc = _att_in(hc, mods, i, cx, g_mix, w_qkv, qg, kg, _identity_tables(t_ctx, LANES), tm_ctx)
            qx, kx, vx = _att_in(xs, mods, i, lat, g_mix, w_qkv, qg, kg, _rope_tables(t_lat, ATT_HEAD_DIM // 4, ATT_ROPE_LAYOUT), tm_lat)
            xs, w_gu, w_down = _attention(sink, qx, kc, vc, kx, vx, w_o, xs, mods, i, lat, ffn_w=ffn_w)
            if with_ctx:
                hc, = _attention(sink, qc, kc, vc, None, None, w_o, hc, mods, i, cx)
        g_ffn = row1(norm_ffn[i])
        xs = _ffn(xs, mods, i, lat, g_ffn, w_gu, w_down, tm_lat)
        if with_ctx:
            hc = _ffn(hc, mods, i, cx, g_ffn, w_gu, w_down, tm_ctx)
    return xs[None]
```

```python
import functools

import jax
import jax.numpy as jnp
from jax import lax
from jax.experimental import pallas as pl
from jax.experimental.pallas import tpu as pltpu

f32 = jnp.float32
bf16 = jnp.bfloat16

D_MODEL = 1024
DEPTH = 4
GRID_W = 64
N_MIXERS = 3
CONV_WIDTH = 31
CONV_HALO = 16
CONV_ROWS = 128
RET_HEADS = 4
RET_QK_DIM = 256
RET_V_DIM = 512
RET_CHUNK = 256
ATT_Q_HEADS = 16
ATT_KV_HEADS = 4
ATT_GROUP = ATT_Q_HEADS // ATT_KV_HEADS
ATT_HEAD_DIM = 64
ATT_WINDOW = 128
ATT_BLOCK = 128
FFN_HIDDEN = 2816
FFN_CHUNK = 256
CAST_GU_STEPS = 32
CAST_DN_STEPS = 16
ROPE_BASE = 10000.0
NORM_EPS = 1e-6
NEG_INF = -1e30
LOG2_E = 1.4426950408889634
LANES = 128
VMEM_LIMIT = 56 * 1024 * 1024

SHIFT_M, SCALE_M, GATE_M, SHIFT_F, SCALE_F, GATE_F = range(6)


def _params(*sem):
    return pltpu.CompilerParams(dimension_semantics=sem, vmem_limit_bytes=VMEM_LIMIT)


def _resident(shape):
    nd = len(shape)
    return pl.BlockSpec(shape, lambda *_: (0,) * nd, pipeline_mode=pl.Buffered(1))


def _mod_spec(layer, which):
    return pl.BlockSpec((None, 8, D_MODEL), lambda *_: (layer, 0, which))


def _modulated(x, gain, scale, shift):
    ms = jnp.mean(x * x, axis=-1, keepdims=True)
    return (x * lax.rsqrt(ms + NORM_EPS)) * (gain * (1.0 + scale)) + shift


def _silu(v):
    return v * jax.nn.sigmoid(v)


def _tile_table(row_ref, col_ref):
    return jnp.concatenate([row_ref[r:r + 1, :] + col_ref[...] for r in range(row_ref.shape[0])], axis=0)


def _table_specs(tm, width):
    row = pl.BlockSpec((tm // GRID_W, width), lambda i: (i, 0))
    col = _resident((GRID_W, width))
    return [row, col, row, col]


def _cast_io(layer):
    gu_rows = D_MODEL // CAST_GU_STEPS
    dn_rows = FFN_HIDDEN // CAST_DN_STEPS
    gu_i = lambda i: jnp.minimum(i, CAST_GU_STEPS - 1)
    dn_i = lambda i: jnp.minimum(i, CAST_DN_STEPS - 1)
    in_specs = [pl.BlockSpec((None, gu_rows, 2 * FFN_HIDDEN), lambda i: (layer, gu_i(i), 0)),
                pl.BlockSpec((None, dn_rows, D_MODEL), lambda i: (layer, dn_i(i), 0))]
    out_specs = [pl.BlockSpec((gu_rows, 2 * FFN_HIDDEN), lambda i: (gu_i(i), 0)),
                 pl.BlockSpec((dn_rows, D_MODEL), lambda i: (dn_i(i), 0))]
    out_shape = [jax.ShapeDtypeStruct((D_MODEL, 2 * FFN_HIDDEN), bf16),
                 jax.ShapeDtypeStruct((FFN_HIDDEN, D_MODEL), bf16)]
    return in_specs, out_specs, out_shape


def _cast_weights(cast_refs):
    if not cast_refs:
        return
    gu_ref, dn_ref, gu_out_ref, dn_out_ref = cast_refs
    i = pl.program_id(0)

    @pl.when(i < CAST_GU_STEPS)
    def _():
        gu_out_ref[...] = gu_ref[...].astype(bf16)

    @pl.when(i < CAST_DN_STEPS)
    def _():
        dn_out_ref[...] = dn_ref[...].astype(bf16)


def _split_refs(refs, n_in, n_out, cast):
    extra = 2 if cast else 0
    ins = refs[:n_in]
    outs = refs[n_in + extra:n_in + extra + n_out]
    scratch = refs[n_in + 2 * extra + n_out:]
    cast_refs = refs[n_in:n_in + extra] + refs[n_in + extra + n_out:n_in + 2 * extra + n_out]
    return ins, outs, scratch, cast_refs


def _adaln_kernel(c_ref, w_ref, b_ref, o_ref):
    s = _silu(c_ref[...])
    o_ref[...] = jnp.dot(s, w_ref[...], precision=lax.Precision.HIGHEST,
                         preferred_element_type=f32) + b_ref[...]


def _adaln(cond8, ada_w, ada_b):
    tn = 1536
    n = 6 * D_MODEL
    return pl.pallas_call(
        _adaln_kernel,
        out_shape=jax.ShapeDtypeStruct((DEPTH, 8, n), f32),
        grid=(DEPTH, n // tn),
        in_specs=[
            pl.BlockSpec((8, D_MODEL), lambda l, j: (0, 0)),
            pl.BlockSpec((None, D_MODEL, tn), lambda l, j: (l, 0, j)),
            pl.BlockSpec((None, 1, tn), lambda l, j: (l, 0, j)),
        ],
        out_specs=pl.BlockSpec((None, 8, tn), lambda l, j: (l, 0, j)),
        compiler_params=_params("arbitrary", "arbitrary"),
        name="adaln",
    )(cond8, ada_w, ada_b.reshape(DEPTH, 1, n))


def _conv_kernel(row, n_tiles, cast, *refs):
    ins, (o_ref,), (xw_ref, win_ref, acc_ref, y_ref), cast_refs = _split_refs(refs, 13, 1, cast)
    xm_ref, xp_ref, xn_ref, g_ref, sh_ref, sc_ref, w1_ref, b1_ref, dw_ref, dwb_ref, ng_ref, w2_ref, gate_ref = ins
    _cast_weights(cast_refs)
    i = pl.program_id(0)
    tm = xm_ref.shape[0]
    rb = CONV_ROWS
    xw_ref[0:CONV_HALO, :] = xp_ref[...]
    xw_ref[CONV_HALO:CONV_HALO + tm, :] = xm_ref[...]
    xw_ref[CONV_HALO + tm:, :] = xn_ref[...]
    h = _modulated(xw_ref[...], g_ref[...], sc_ref[row:row + 1, :], sh_ref[row:row + 1, :])
    y = jnp.dot(h.astype(bf16), w1_ref[...], preferred_element_type=f32) + b1_ref[...]
    win_ref[...] = y[:, :D_MODEL] * jax.nn.sigmoid(y[:, D_MODEL:])

    @pl.when(i == 0)
    def _():
        win_ref[0:CONV_HALO, :] = jnp.zeros((CONV_HALO, D_MODEL), f32)

    @pl.when(i == n_tiles - 1)
    def _():
        win_ref[CONV_HALO + tm:, :] = jnp.zeros((CONV_HALO, D_MODEL), f32)

    off = CONV_HALO - CONV_WIDTH // 2

    def body(r, carry):
        base = pl.multiple_of(r * rb, rb)
        for lb in range(D_MODEL // LANES):
            ls = slice(lb * LANES, (lb + 1) * LANES)
            win = win_ref[pl.ds(base, rb + 2 * CONV_HALO), ls]
            out = jnp.broadcast_to(dwb_ref[:, ls], (rb, LANES))
            for s in range(8):
                z = None
                for k in range(CONV_WIDTH):
                    if (off + k) % 8 != s:
                        continue
                    j = (off + k) // 8
                    term = win[8 * j:8 * j + rb + 8, :] * dw_ref[k:k + 1, ls]
                    z = term if z is None else z + term
                if z is None:
                    continue
                out = out + (z[0:rb] if s == 0 else pltpu.roll(z, rb + 8 - s, axis=0)[0:rb])
            acc_ref[:, ls] = out
        acc = acc_ref[...]
        mu = jnp.mean(acc, axis=-1, keepdims=True)
        xc = acc - mu
        var = jnp.mean(xc * xc, axis=-1, keepdims=True)
        yn = xc * lax.rsqrt(var + NORM_EPS) * ng_ref[...]
        y_ref[pl.ds(base, rb), :] = _silu(yn).astype(bf16)
        return carry

    lax.fori_loop(0, tm // rb, body, 0)
    o = jnp.dot(y_ref[...], w2_ref[...], preferred_element_type=f32)
    o_ref[...] = xm_ref[...] + gate_ref[row:row + 1, :] * o


def _conv_mixer(x, mods, layer, row, gain, w1, b1, dw, dw_b, norm_g, w2, tm, ffn_w=None):
    t = x.shape[0]
    cast = ffn_w is not None
    c_in, c_out, c_shape = _cast_io(layer) if cast else ([], [], [])
    n_tiles = t // tm
    hb = tm // CONV_HALO
    n_hb = t // CONV_HALO
    return pl.pallas_call(
        functools.partial(_conv_kernel, row, n_tiles, cast),
        out_shape=[jax.ShapeDtypeStruct((t, D_MODEL), f32)] + c_shape,
        grid=(n_tiles,),
        in_specs=[
            pl.BlockSpec((tm, D_MODEL), lambda i: (i, 0)),
            pl.BlockSpec((CONV_HALO, D_MODEL), lambda i: (jnp.maximum(i * hb - 1, 0), 0)),
            pl.BlockSpec((CONV_HALO, D_MODEL), lambda i: (jnp.minimum((i + 1) * hb, n_hb - 1), 0)),
            _resident((1, D_MODEL)),
            _mod_spec(layer, SHIFT_M),
            _mod_spec(layer, SCALE_M),
            _resident((D_MODEL, 2 * D_MODEL)),
            _resident((1, 2 * D_MODEL)),
            _resident((CONV_WIDTH, D_MODEL)),
            _resident((1, D_MODEL)),
            _resident((1, D_MODEL)),
            _resident((D_MODEL, D_MODEL)),
            _mod_spec(layer, GATE_M),
        ] + c_in,
        out_specs=[pl.BlockSpec((tm, D_MODEL), lambda i: (i, 0))] + c_out,
        scratch_shapes=[
            pltpu.VMEM((tm + 2 * CONV_HALO, D_MODEL), f32),
            pltpu.VMEM((tm + 2 * CONV_HALO, D_MODEL), f32),
            pltpu.VMEM((CONV_ROWS, D_MODEL), f32),
            pltpu.VMEM((tm, D_MODEL), bf16),
        ],
        compiler_params=_params("arbitrary"),
        name="conv_mixer",
    )(x, x, x, gain, mods, mods, w1, b1, dw, dw_b, norm_g, w2, mods, *(ffn_w or ()))


def _ret_in_kernel(row, x_ref, g_ref, sh_ref, sc_ref, w_ref, rcos_ref, ccos_ref, rsin_ref, csin_ref,
                   q_ref, k_ref, v_ref, sg_ref):
    h = _modulated(x_ref[...], g_ref[...], sc_ref[row:row + 1, :], sh_ref[row:row + 1, :]).astype(bf16)
    hk = RET_HEADS * RET_QK_DIM
    hv = RET_HEADS * RET_V_DIM
    cos_t = _tile_table(rcos_ref, ccos_ref)
    sin_t = _tile_table(rsin_ref, csin_ref)

    def rope(y, scale):
        outs = []
        for b in range(hk // LANES):
            yb = y[:, b * LANES:(b + 1) * LANES]
            tb = (b % 2) * LANES
            rot = yb * cos_t[:, tb:tb + LANES] + pltpu.roll(yb, LANES // 2, axis=1) * sin_t[:, tb:tb + LANES]
            outs.append(rot * scale)
        return jnp.concatenate(outs, axis=1)

    q = jnp.dot(h, w_ref[:, 0:hk], preferred_element_type=f32)
    q_ref[...] = rope(q, 1.0).astype(bf16)
    k = jnp.dot(h, w_ref[:, hk:2 * hk], preferred_element_type=f32)
    k_ref[...] = rope(k, RET_QK_DIM ** -0.5).astype(bf16)
    v_ref[...] = jnp.dot(h, w_ref[:, 2 * hk:2 * hk + hv], preferred_element_type=f32).astype(bf16)
    g = jnp.dot(h, w_ref[:, 2 * hk + hv:], preferred_element_type=f32)
    sg_ref[...] = _silu(g).astype(bf16)


def _ret_in(x, mods, layer, row, gain, w_in, tabs, tm):
    t = x.shape[0]
    hk = RET_HEADS * RET_QK_DIM
    hv = RET_HEADS * RET_V_DIM
    tok = lambda n: pl.BlockSpec((tm, n), lambda i: (i, 0))
    return pl.pallas_call(
        functools.partial(_ret_in_kernel, row),
        out_shape=[jax.ShapeDtypeStruct((t, hk), bf16), jax.ShapeDtypeStruct((t, hk), bf16),
                   jax.ShapeDtypeStruct((t, hv), bf16), jax.ShapeDtypeStruct((t, hv), bf16)],
        grid=(t // tm,),
        in_specs=[
            tok(D_MODEL),
            _resident((1, D_MODEL)),
            _mod_spec(layer, SHIFT_M),
            _mod_spec(layer, SCALE_M),
            _resident((D_MODEL, 2 * hk + 2 * hv)),
            *_table_specs(tm, RET_QK_DIM),
        ],
        out_specs=[tok(hk), tok(hk), tok(hv), tok(hv)],
        compiler_params=_params("arbitrary"),
        name="ret_in",
    )(x, gain, mods, mods, w_in, *tabs)


def _log_sigmoid(v):
    return jnp.minimum(v, 0.0) - jnp.log1p(jnp.exp(-jnp.abs(v)))


def _ret_state_kernel(k_ref, v_ref, decb_ref, s0b_ref, sball_ref, sb_ref):
    c = RET_CHUNK

    @pl.when(pl.program_id(0) == 0)
    def _():
        sb_ref[...] = s0b_ref[...]

    idx = lax.broadcasted_iota(jnp.int32, (c, 1), 0).astype(f32)
    for h in range(RET_HEADS):
        lg_b = _log_sigmoid(decb_ref[h, 0:1, :])
        kd = jnp.exp(lg_b[:, 0:1] * idx)
        kh = (k_ref[:, h * RET_QK_DIM:(h + 1) * RET_QK_DIM].astype(f32) * kd).astype(bf16)
        a = lax.dot_general(kh, v_ref[:, h * RET_V_DIM:(h + 1) * RET_V_DIM], (((0,), (0,)), ((), ())),
                            preferred_element_type=f32)
        s = sb_ref[h]
        sball_ref[h] = s.astype(bf16)
        sb_ref[h] = s * jnp.exp(lg_b * c) + a


def _ret_state(k, v, dec_b, s0b):
    t = k.shape[0]
    n = t // RET_CHUNK
    st = (RET_HEADS, RET_QK_DIM, RET_V_DIM)
    bwd = lambda w: pl.BlockSpec((RET_CHUNK, w), lambda i: (n - 1 - i, 0))
    return pl.pallas_call(
        _ret_state_kernel,
        out_shape=[jax.ShapeDtypeStruct((n,) + st, bf16), jax.ShapeDtypeStruct(st, f32)],
        grid=(n,),
        in_specs=[bwd(RET_HEADS * RET_QK_DIM), bwd(RET_HEADS * RET_V_DIM),
                  _resident((RET_HEADS, 8, RET_V_DIM)), _resident(st)],
        out_specs=[pl.BlockSpec((None,) + st, lambda i: (n - 1 - i, 0, 0, 0)),
                   pl.BlockSpec(st, lambda i: (0, 0, 0))],
        compiler_params=_params("arbitrary"),
        name="ret_state",
    )(k, v, dec_b, s0b)


def _ret_out_kernel(row, cast, *refs):
    ins, (o_ref, sf_ref), (dec_ref, y_ref), cast_refs = _split_refs(refs, 11, 2, cast)
    q_ref, k_ref, v_ref, sg_ref, sb_ref, s0f_ref, decf_ref, decb_ref, w_ref, x_ref, gate_ref = ins
    _cast_weights(cast_refs)
    c = RET_CHUNK

    @pl.when(pl.program_id(0) == 0)
    def _():
        sf_ref[...] = s0f_ref[...]
        t_i = lax.broadcasted_iota(jnp.int32, (c, c), 0)
        m_i = lax.broadcasted_iota(jnp.int32, (c, c), 1)
        rel = (t_i - m_i).astype(f32)
        for h in range(RET_HEADS):
            lg_f = _log_sigmoid(decf_ref[h, 0:1, :c])
            lg_b = _log_sigmoid(decb_ref[h, 0:1, :c])
            d_f = jnp.where(rel >= 0, jnp.exp(lg_f * jnp.maximum(rel, 0.0)), 0.0)
            d_b = jnp.where(rel <= 0, jnp.exp(lg_b * jnp.maximum(-rel, 0.0)), 0.0)
            dec_ref[h] = d_f + d_b

    idx = lax.broadcasted_iota(jnp.int32, (c, 1), 0).astype(f32)
    for h in range(RET_HEADS):
        lg_f = _log_sigmoid(decf_ref[h, 0:1, :])
        lg_b = _log_sigmoid(decb_ref[h, 0:1, :])
        q = q_ref[:, h * RET_QK_DIM:(h + 1) * RET_QK_DIM]
        k = k_ref[:, h * RET_QK_DIM:(h + 1) * RET_QK_DIM]
        v = v_ref[:, h * RET_V_DIM:(h + 1) * RET_V_DIM]
        s_f = sf_ref[h]
        s = lax.dot_general(q, k, (((1,), (1,)), ((), ())), preferred_element_type=f32)
        o = jnp.dot((s * dec_ref[h]).astype(bf16), v, preferred_element_type=f32)
        o = o + jnp.exp(lg_f[:, 0:1] * (idx + 1.0)) * jnp.dot(q, s_f.astype(bf16), preferred_element_type=f32)
        o = o + jnp.exp(lg_b[:, 0:1] * (c - idx)) * jnp.dot(q, sb_ref[h], preferred_element_type=f32)
        o = o * lax.rsqrt(jnp.mean(o * o, axis=-1, keepdims=True) + NORM_EPS)
        sg = sg_ref[:, h * RET_V_DIM:(h + 1) * RET_V_DIM].astype(f32)
        y_ref[:, h * RET_V_DIM:(h + 1) * RET_V_DIM] = (sg * o).astype(bf16)
        kd = (k.astype(f32) * jnp.exp(lg_f[:, 0:1] * (c - 1.0 - idx))).astype(bf16)
        a = lax.dot_general(kd, v, (((0,), (0,)), ((), ())), preferred_element_type=f32)
        sf_ref[h] = s_f * jnp.exp(lg_f * c) + a
    out = jnp.dot(y_ref[...], w_ref[...], preferred_element_type=f32)
    o_ref[...] = x_ref[...] + gate_ref[row:row + 1, :] * out


def _ret_out(q, k, v, sg, sb_all, s0f, dec_f, dec_b, w_out, x, mods, layer, row, ffn_w=None):
    t = x.shape[0]
    cast = ffn_w is not None
    c_in, c_out, c_shape = _cast_io(layer) if cast else ([], [], [])
    n = t // RET_CHUNK
    hk = RET_HEADS * RET_QK_DIM
    hv = RET_HEADS * RET_V_DIM
    st = (RET_HEADS, RET_QK_DIM, RET_V_DIM)
    tok = lambda w: pl.BlockSpec((RET_CHUNK, w), lambda i: (i, 0))
    return pl.pallas_call(
        functools.partial(_ret_out_kernel, row, cast),
        out_shape=[jax.ShapeDtypeStruct((t, D_MODEL), f32), jax.ShapeDtypeStruct(st, f32)] + c_shape,
        grid=(n,),
        in_specs=[tok(hk), tok(hk), tok(hv), tok(hv),
                  pl.BlockSpec((None,) + st, lambda i: (i, 0, 0, 0)),
                  _resident(st),
                  _resident((RET_HEADS, 8, RET_V_DIM)), _resident((RET_HEADS, 8, RET_V_DIM)),
                  _resident((hv, D_MODEL)),
                  tok(D_MODEL),
                  _mod_spec(layer, GATE_M)] + c_in,
        out_specs=[tok(D_MODEL), pl.BlockSpec(st, lambda i: (0, 0, 0))] + c_out,
        scratch_shapes=[pltpu.VMEM((RET_HEADS, RET_CHUNK, RET_CHUNK), f32),
                        pltpu.VMEM((RET_CHUNK, hv), bf16)],
        compiler_params=_params("arbitrary"),
        name="ret_out",
    )(q, k, v, sg, sb_all, s0f, dec_f, dec_b, w_out, x, mods, *(ffn_w or ()))


def _att_in_kernel(row, x_ref, g_ref, sh_ref, sc_ref, w_ref, wvt_ref, qg_ref, kg_ref,
                   rcos_ref, ccos_ref, rsin_ref, csin_ref, q_ref, k_ref, vt_ref):
    h = _modulated(x_ref[...], g_ref[...], sc_ref[row:row + 1, :], sh_ref[row:row + 1, :]).astype(bf16)
    nq = ATT_Q_HEADS * LANES
    nk = ATT_KV_HEADS * LANES
    cos_t = _tile_table(rcos_ref, ccos_ref)
    sin_t = _tile_table(rsin_ref, csin_ref)

    def partner(t):
        return pltpu.roll(t, LANES // 2, axis=1)

    def tables(gain_ref, scale):
        gain = jnp.broadcast_to(gain_ref[...], (8, LANES))
        return cos_t * (gain[0:1, :] * scale), sin_t * (partner(gain)[0:1, :] * scale)

    def head_norm_rope(y, cos_g, sin_g):
        ms = jnp.sum(y * y, axis=-1, keepdims=True) * (1.0 / ATT_HEAD_DIM)
        return (y * cos_g + partner(y) * sin_g) * lax.rsqrt(ms + NORM_EPS)

    q_tabs = tables(qg_ref, ATT_HEAD_DIM ** -0.5 * LOG2_E)
    k_tabs = tables(kg_ref, 1.0)
    for c0 in range(0, nq + nk, 2 * LANES):
        y = jnp.dot(h, w_ref[:, c0:c0 + 2 * LANES], preferred_element_type=f32)
        for col in (c0, c0 + LANES):
            yh = y[:, col - c0:col - c0 + LANES]
            if col < nq:
                q_ref[:, col:col + LANES] = head_norm_rope(yh, *q_tabs).astype(bf16)
            else:
                k_ref[:, col - nq:col - nq + LANES] = head_norm_rope(yh, *k_tabs).astype(bf16)
    vt_ref[...] = lax.dot_general(wvt_ref[...], h, (((1,), (1,)), ((), ())), preferred_element_type=f32).astype(bf16)


def _att_in(x, mods, layer, row, gain, w_qk_p, w_v_t, q_gain_p, k_gain_p, tabs, tm):
    t = x.shape[0]
    nq = ATT_Q_HEADS * LANES
    nk = ATT_KV_HEADS * LANES
    nv = ATT_KV_HEADS * ATT_HEAD_DIM
    tok = lambda n: pl.BlockSpec((tm, n), lambda i: (i, 0))
    return pl.pallas_call(
        functools.partial(_att_in_kernel, row),
        out_shape=[jax.ShapeDtypeStruct((t, nq), bf16), jax.ShapeDtypeStruct((t, nk), bf16),
                   jax.ShapeDtypeStruct((nv, t), bf16)],
        grid=(t // tm,),
        in_specs=[
            tok(D_MODEL),
            _resident((1, D_MODEL)),
            _mod_spec(layer, SHIFT_M),
            _mod_spec(layer, SCALE_M),
            _resident((D_MODEL, nq + nk)),
            _resident((nv, D_MODEL)),
            _resident((1, LANES)),
            _resident((1, LANES)),
            *_table_specs(tm, LANES),
        ],
        out_specs=[tok(nq), tok(nk), pl.BlockSpec((nv, tm), lambda i: (0, i))],
        compiler_params=_params("arbitrary"),
        name="att_in",
    )(x, gain, mods, mods, w_qk_p, w_v_t, q_gain_p, k_gain_p, *tabs)


def _att_kernel(row, n_blocks, band, cast, *refs):
    ins, (o_ref,), scratch, cast_refs = _split_refs(refs, 13 if band else 7, 1, cast)
    s_refs, e_refs, yt_ref = scratch[:ATT_KV_HEADS], scratch[ATT_KV_HEADS:2 * ATT_KV_HEADS], scratch[-1]
    if band:
        sink_ref, q_ref, kc_ref, vtc_ref, kp_ref, kn_ref, kx_ref, vtp_ref, vtn_ref, vtx_ref, w_ref, x_ref, gate_ref = ins
    else:
        sink_ref, q_ref, kc_ref, vtc_ref, w_ref, x_ref, gate_ref = ins
    _cast_weights(cast_refs)
    i = pl.program_id(0)
    c = ATT_BLOCK
    g = ATT_GROUP
    hd = ATT_HEAD_DIM
    n_ctx = kc_ref.shape[0]
    if band:
        key = lax.broadcasted_iota(jnp.int32, (c, g * c), 0)
        qry = lax.broadcasted_iota(jnp.int32, (c, g * c), 1) % c
        prev_ok = (key >= qry + (c - ATT_WINDOW)) & (i > 0)
        next_ok = (key <= qry + (ATT_WINDOW - c)) & (i < n_blocks - 1)
    vals_t = []
    for kh in range(ATT_KV_HEADS):
        sl = slice(kh * LANES, (kh + 1) * LANES)
        vs = slice(kh * hd, (kh + 1) * hd)
        q = jnp.concatenate([q_ref[:, (kh * g + gg) * LANES:(kh * g + gg + 1) * LANES] for gg in range(g)], axis=0)
        if band:
            keys = jnp.concatenate([kc_ref[:, sl], kp_ref[:, sl], kx_ref[:, sl], kn_ref[:, sl]], axis=0)
            vals_t.append(jnp.concatenate([vtc_ref[vs, :], vtp_ref[vs, :], vtx_ref[vs, :], vtn_ref[vs, :]], axis=1))
        else:
            keys = kc_ref[:, sl]
            vals_t.append(vtc_ref[vs, :])
        s_refs[kh][...] = lax.dot_general(keys, q, (((1,), (1,)), ((), ())), preferred_element_type=f32)
    for kh in range(ATT_KV_HEADS):
        s = s_refs[kh][...]
        if band:
            s = jnp.concatenate([
                s[:n_ctx],
                jnp.where(prev_ok, s[n_ctx:n_ctx + c], NEG_INF),
                s[n_ctx + c:n_ctx + 2 * c],
                jnp.where(next_ok, s[n_ctx + 2 * c:], NEG_INF)], axis=0)
        sink = jnp.concatenate([jnp.full((1, c), sink_ref[kh * g + gg] * LOG2_E, f32) for gg in range(g)], axis=1)
        m = jnp.maximum(jnp.max(s, axis=0, keepdims=True), sink)
        e = jnp.exp2(s - m)
        denom = jnp.sum(e, axis=0, keepdims=True) + jnp.exp2(sink - m)
        e_refs[kh][...] = e.astype(bf16)
        o_t = jnp.dot(vals_t[kh], e_refs[kh][...], preferred_element_type=f32) * (1.0 / denom)
        for gg in range(g):
            h = kh * g + gg
            yt_ref[h * hd:(h + 1) * hd, :] = o_t[:, gg * c:(gg + 1) * c].astype(bf16)
    out = lax.dot_general(yt_ref[...], w_ref[...], (((0,), (0,)), ((), ())), preferred_element_type=f32)
    o_ref[...] = x_ref[...] + gate_ref[row:row + 1, :] * out


def _attention(sink, q, kc, vtc, kx, vtx, w_o, x, mods, layer, row, ffn_w=None):
    t = x.shape[0]
    cast = ffn_w is not None
    c_in, c_out, c_shape = _cast_io(layer) if cast else ([], [], [])
    c = ATT_BLOCK
    n = t // c
    nq = ATT_Q_HEADS * LANES
    nk = ATT_KV_HEADS * LANES
    nv = ATT_KV_HEADS * ATT_HEAD_DIM
    band = kx is not None
    n_keys = kc.shape[0] + (3 * c if band else 0)
    tok = lambda w: pl.BlockSpec((c, w), lambda i: (i, 0))
    before = lambda i: jnp.maximum(i - 1, 0)
    after = lambda i: jnp.minimum(i + 1, n - 1)
    in_specs = [pl.BlockSpec(memory_space=pltpu.SMEM), tok(nq),
                _resident(kc.shape), _resident(vtc.shape)]
    args = [sink, q, kc, vtc]
    if band:
        in_specs += [pl.BlockSpec((c, nk), lambda i: (before(i), 0)), pl.BlockSpec((c, nk), lambda i: (after(i), 0)),
                     tok(nk),
                     pl.BlockSpec((nv, c), lambda i: (0, before(i))), pl.BlockSpec((nv, c), lambda i: (0, after(i))),
                     pl.BlockSpec((nv, c), lambda i: (0, i))]
        args += [kx, kx, kx, vtx, vtx, vtx]
    in_specs += [_resident((ATT_Q_HEADS * ATT_HEAD_DIM, D_MODEL)), tok(D_MODEL), _mod_spec(layer, GATE_M)] + c_in
    args += [w_o, x, mods, *(ffn_w or ())]
    return pl.pallas_call(
        functools.partial(_att_kernel, row, n, band, cast),
        out_shape=[jax.ShapeDtypeStruct((t, D_MODEL), f32)] + c_shape,
        grid=(n,),
        in_specs=in_specs,
        out_specs=[tok(D_MODEL)] + c_out,
        scratch_shapes=([pltpu.VMEM((n_keys, ATT_GROUP * c), f32)] * ATT_KV_HEADS
                        + [pltpu.VMEM((n_keys, ATT_GROUP * c), bf16)] * ATT_KV_HEADS
                        + [pltpu.VMEM((ATT_Q_HEADS * ATT_HEAD_DIM, c), bf16)]),
        compiler_params=_params("arbitrary"),
        name="att_band" if band else "att_ctx",
    )(*args)


def _ffn_kernel(row, x_ref, g_ref, sh_ref, sc_ref, gate_ref, wgu_ref, wd_ref, o_ref):
    x = x_ref[...]
    h = _modulated(x, g_ref[...], sc_ref[row:row + 1, :], sh_ref[row:row + 1, :]).astype(bf16)
    acc = jnp.zeros(x.shape, f32)
    for c0 in range(0, FFN_HIDDEN, FFN_CHUNK):
        a = jnp.dot(h, wgu_ref[:, c0:c0 + FFN_CHUNK], preferred_element_type=f32)
        b = jnp.dot(h, wgu_ref[:, FFN_HIDDEN + c0:FFN_HIDDEN + c0 + FFN_CHUNK], preferred_element_type=f32)
        act = (_silu(a) * b).astype(bf16)
        acc = acc + jnp.dot(act, wd_ref[c0:c0 + FFN_CHUNK, :], preferred_element_type=f32)
    o_ref[...] = x + gate_ref[row:row + 1, :] * acc


def _ffn(x, mods, layer, row, gain, w_gu, w_down, tm):
    t = x.shape[0]
    tok = pl.BlockSpec((tm, D_MODEL), lambda i: (i, 0))
    return pl.pallas_call(
        functools.partial(_ffn_kernel, row),
        out_shape=jax.ShapeDtypeStruct((t, D_MODEL), f32),
        grid=(t // tm,),
        in_specs=[tok, _resident((1, D_MODEL)),
                  _mod_spec(layer, SHIFT_F), _mod_spec(layer, SCALE_F), _mod_spec(layer, GATE_F),
                  _resident((D_MODEL, 2 * FFN_HIDDEN)), _resident((FFN_HIDDEN, D_MODEL))],
        out_specs=tok,
        compiler_params=_params("arbitrary"),
        name="ffn",
    )(x, gain, mods, mods, mods, w_gu, w_down)


def _rope_tables(t, quarter, layout):
    n_rows = t // GRID_W
    inv = ROPE_BASE ** (-jnp.arange(quarter, dtype=f32) / quarter)
    ang = {'r': jnp.arange(n_rows).astype(f32)[:, None] * inv, 'c': jnp.arange(GRID_W).astype(f32)[:, None] * inv}
    n = {'r': n_rows, 'c': GRID_W}

    def table(axis, fn, signed):
        parts = []
        for grp in layout:
            if isinstance(grp, int):
                parts.append(jnp.zeros((n[axis], grp), f32))
            elif grp[0] == axis:
                sign = -1.0 if (signed and grp[1] == '1') else 1.0
                parts.append(sign * fn(ang[axis]))
            else:
                parts.append(jnp.zeros((n[axis], quarter), f32))
        return jnp.concatenate(parts, axis=1)

    return (table('r', jnp.cos, False), table('c', jnp.cos, False),
            table('r', jnp.sin, True), table('c', jnp.sin, True))


RET_ROPE_LAYOUT = ('r1', 'r2', 'c1', 'c2')
ATT_ROPE_LAYOUT = ('r1', 'c1', LANES // 2 - ATT_HEAD_DIM // 2, 'r2', 'c2', LANES // 2 - ATT_HEAD_DIM // 2)


def _identity_tables(t, width):
    n_rows = t // GRID_W
    return (jnp.ones((n_rows, width), f32), jnp.zeros((GRID_W, width), f32),
            jnp.zeros((n_rows, width), f32), jnp.zeros((GRID_W, width), f32))


def _pad_heads_rope(w, n_heads):
    lead = w.shape[:-1]
    q4 = ATT_HEAD_DIM // 4
    w = w.reshape(lead + (n_heads, ATT_HEAD_DIM))
    z = jnp.zeros(lead + (n_heads, LANES // 2 - 2 * q4), w.dtype)
    w = jnp.concatenate([w[..., 0:q4], w[..., 2 * q4:3 * q4], z, w[..., q4:2 * q4], w[..., 3 * q4:], z], axis=-1)
    return w.reshape(lead + (n_heads * LANES,))


def kernel(x, c, ctx, c_ctx, ada_w, ada_b, norm_mix, norm_ffn, conv_w1, conv_b1, conv_dw, conv_dw_b, conv_norm, conv_w2, ret_w_in, ret_decay_f, ret_decay_b, ret_w_out, att_w_qkv, att_q_norm, att_k_norm, att_sink, att_w_o, ffn_w_gu, ffn_w_down):
    assert x.shape[0] == 1 and c.shape[0] == 1 and ctx.shape[0] == 1
    t_lat, t_ctx = x.shape[1], ctx.shape[1]
    tm_lat, tm_ctx = 512, t_ctx
    xs, hc = x[0], ctx[0]
    lat, cx = 0, 1

    cond8 = jnp.zeros((8, D_MODEL), f32).at[lat].set(c[0]).at[cx].set(c_ctx)
    mods = _adaln(cond8, ada_w, ada_b)

    row1 = lambda v: v.reshape(1, -1)
    ffn_w = (ffn_w_gu, ffn_w_down)
    for i in range(DEPTH):
        kind, j, last = i % N_MIXERS, i // N_MIXERS, i == DEPTH - 1
        with_ctx = not last
        g_mix = row1(norm_mix[i])
        if kind == 0:
            w1, w2 = conv_w1[j].astype(bf16), conv_w2[j].astype(bf16)
            cargs = (g_mix, w1, row1(conv_b1[j]), conv_dw[j], row1(conv_dw_b[j]), row1(conv_norm[j]), w2)
            xs, w_gu, w_down = _conv_mixer(xs, mods, i, lat, *cargs, tm_lat, ffn_w=ffn_w)
            if with_ctx:
                hc, = _conv_mixer(hc, mods, i, cx, *cargs, tm_ctx)
        elif kind == 1:
            w_in, w_out = ret_w_in[j].astype(bf16), ret_w_out[j].astype(bf16)
            bdec = lambda d: jnp.broadcast_to(d[:, None, None], (RET_HEADS, 8, RET_V_DIM)).astype(f32)
            dec_f, dec_b = bdec(ret_decay_f[j]), bdec(ret_decay_b[j])
            zeros = jnp.zeros((RET_HEADS, RET_QK_DIM, RET_V_DIM), f32)
            qc, kc, vc, gc = _ret_in(hc, mods, i, cx, g_mix, w_in, _identity_tables(t_ctx, RET_QK_DIM), tm_ctx)
            sb_c, s0b = _ret_state(kc, vc, dec_b, zeros)
            hc_new, s0f = _ret_out(qc, kc, vc, gc, sb_c, zeros, dec_f, dec_b, w_out, hc, mods, i, cx)
            qx, kx, vx, gx = _ret_in(xs, mods, i, lat, g_mix, w_in, _rope_tables(t_lat, RET_QK_DIM // 4, RET_ROPE_LAYOUT), tm_lat)
            sb_x, _ = _ret_state(kx, vx, dec_b, s0b)
            xs, _, w_gu, w_down = _ret_out(qx, kx, vx, gx, sb_x, s0f, dec_f, dec_b, w_out, xs, mods, i, lat, ffn_w=ffn_w)
            if with_ctx:
                hc = hc_new
        else:
            n_qk = (ATT_Q_HEADS + ATT_KV_HEADS) * ATT_HEAD_DIM
            w_qk = _pad_heads_rope(att_w_qkv[j][:, :n_qk], ATT_Q_HEADS + ATT_KV_HEADS).astype(bf16)
            w_v_t = att_w_qkv[j][:, n_qk:].T.astype(bf16)
            w_o = att_w_o[j].astype(bf16)
            qg, kg = _pad_heads_rope(row1(att_q_norm[j]), 1), _pad_heads_rope(row1(att_k_norm[j]), 1)
            sink = att_sink[j].astype(f32)
            qc, kc, vtc = _att_in(hc, mods, i, cx, g_mix, w_qk, w_v_t, qg, kg, _identity_tables(t_ctx, LANES), tm_ctx)
            qx, kx, vtx = _att_in(xs, mods, i, lat, g_mix, w_qk, w_v_t, qg, kg, _rope_tables(t_lat, ATT_HEAD_DIM // 4, ATT_ROPE_LAYOUT), tm_lat)
            xs, w_gu, w_down = _attention(sink, qx, kc, vtc, kx, vtx, w_o, xs, mods, i, lat, ffn_w=ffn_w)
            if with_ctx:
                hc, = _attention(sink, qc, kc, vtc, None, None, w_o, hc, mods, i, cx)
        g_ffn = row1(norm_ffn[i])
        xs = _ffn(xs, mods, i, lat, g_ffn, w_gu, w_down, tm_lat)
        if with_ctx:
            hc = _ffn(hc, mods, i, cx, g_ffn, w_gu, w_down, tm_ctx)
    return xs[None]
```

```python
import functools

import jax
import jax.numpy as jnp
from jax import lax
from jax.experimental import pallas as pl
from jax.experimental.pallas import tpu as pltpu

f32 = jnp.float32
bf16 = jnp.bfloat16

D_MODEL = 1024
DEPTH = 4
GRID_W = 64
N_MIXERS = 3
CONV_WIDTH = 31
CONV_HALO = 16
CONV_ROWS = 128
RET_HEADS = 4
RET_QK_DIM = 256
RET_V_DIM = 512
RET_CHUNK = 256
ATT_Q_HEADS = 16
ATT_KV_HEADS = 4
ATT_GROUP = ATT_Q_HEADS // ATT_KV_HEADS
ATT_HEAD_DIM = 64
ATT_WINDOW = 128
ATT_BLOCK = 128
FFN_HIDDEN = 2816
FFN_CHUNK = 256
CAST_GU_STEPS = 32
CAST_DN_STEPS = 16
ROPE_BASE = 10000.0
NORM_EPS = 1e-6
NEG_INF = -1e30
LOG2_E = 1.4426950408889634
LANES = 128
VMEM_LIMIT = 56 * 1024 * 1024

SHIFT_M, SCALE_M, GATE_M, SHIFT_F, SCALE_F, GATE_F = range(6)


def _params(*sem):
    return pltpu.CompilerParams(dimension_semantics=sem, vmem_limit_bytes=VMEM_LIMIT)


def _resident(shape):
    nd = len(shape)
    return pl.BlockSpec(shape, lambda *_: (0,) * nd, pipeline_mode=pl.Buffered(1))


def _mod_spec(layer, which):
    return pl.BlockSpec((None, 8, D_MODEL), lambda *_: (layer, 0, which))


def _modulated(x, gain, scale, shift):
    ms = jnp.mean(x * x, axis=-1, keepdims=True)
    return (x * lax.rsqrt(ms + NORM_EPS)) * (gain * (1.0 + scale)) + shift


def _silu(v):
    return v * jax.nn.sigmoid(v)


def _tile_table(row_ref, col_ref):
    return jnp.concatenate([row_ref[r:r + 1, :] + col_ref[...] for r in range(row_ref.shape[0])], axis=0)


def _table_specs(tm, width):
    row = pl.BlockSpec((tm // GRID_W, width), lambda i: (i, 0))
    col = _resident((GRID_W, width))
    return [row, col, row, col]


def _cast_io(layer):
    gu_rows = D_MODEL // CAST_GU_STEPS
    dn_rows = FFN_HIDDEN // CAST_DN_STEPS
    gu_i = lambda i: jnp.minimum(i, CAST_GU_STEPS - 1)
    dn_i = lambda i: jnp.minimum(i, CAST_DN_STEPS - 1)
    in_specs = [pl.BlockSpec((None, gu_rows, 2 * FFN_HIDDEN), lambda i: (layer, gu_i(i), 0)),
                pl.BlockSpec((None, dn_rows, D_MODEL), lambda i: (layer, dn_i(i), 0))]
    out_specs = [pl.BlockSpec((gu_rows, 2 * FFN_HIDDEN), lambda i: (gu_i(i), 0)),
                 pl.BlockSpec((dn_rows, D_MODEL), lambda i: (dn_i(i), 0))]
    out_shape = [jax.ShapeDtypeStruct((D_MODEL, 2 * FFN_HIDDEN), bf16),
                 jax.ShapeDtypeStruct((FFN_HIDDEN, D_MODEL), bf16)]
    return in_specs, out_specs, out_shape


def _cast_weights(cast_refs):
    if not cast_refs:
        return
    gu_ref, dn_ref, gu_out_ref, dn_out_ref = cast_refs
    i = pl.program_id(0)

    @pl.when(i < CAST_GU_STEPS)
    def _():
        gu_out_ref[...] = gu_ref[...].astype(bf16)

    @pl.when(i < CAST_DN_STEPS)
    def _():
        dn_out_ref[...] = dn_ref[...].astype(bf16)


def _split_refs(refs, n_in, n_out, cast):
    extra = 2 if cast else 0
    ins = refs[:n_in]
    outs = refs[n_in + extra:n_in + extra + n_out]
    scratch = refs[n_in + 2 * extra + n_out:]
    cast_refs = refs[n_in:n_in + extra] + refs[n_in + extra + n_out:n_in + 2 * extra + n_out]
    return ins, outs, scratch, cast_refs


def _adaln_kernel(c_ref, w_ref, b_ref, o_ref):
    s = _silu(c_ref[...])
    o_ref[...] = jnp.dot(s, w_ref[...], precision=lax.Precision.HIGHEST,
                         preferred_element_type=f32) + b_ref[...]


def _adaln(cond8, ada_w, ada_b):
    tn = 1536
    n = 6 * D_MODEL
    return pl.pallas_call(
        _adaln_kernel,
        out_shape=jax.ShapeDtypeStruct((DEPTH, 8, n), f32),
        grid=(DEPTH, n // tn),
        in_specs=[
            pl.BlockSpec((8, D_MODEL), lambda l, j: (0, 0)),
            pl.BlockSpec((None, D_MODEL, tn), lambda l, j: (l, 0, j)),
            pl.BlockSpec((None, 1, tn), lambda l, j: (l, 0, j)),
        ],
        out_specs=pl.BlockSpec((None, 8, tn), lambda l, j: (l, 0, j)),
        compiler_params=_params("arbitrary", "arbitrary"),
        name="adaln",
    )(cond8, ada_w, ada_b.reshape(DEPTH, 1, n))


def _conv_kernel(row, n_tiles, cast, *refs):
    ins, (o_ref,), (xw_ref, win_ref, acc_ref, y_ref), cast_refs = _split_refs(refs, 13, 1, cast)
    xm_ref, xp_ref, xn_ref, g_ref, sh_ref, sc_ref, w1_ref, b1_ref, dw_ref, dwb_ref, ng_ref, w2_ref, gate_ref = ins
    _cast_weights(cast_refs)
    i = pl.program_id(0)
    tm = xm_ref.shape[0]
    rb = CONV_ROWS
    xw_ref[0:CONV_HALO, :] = xp_ref[...]
    xw_ref[CONV_HALO:CONV_HALO + tm, :] = xm_ref[...]
    xw_ref[CONV_HALO + tm:, :] = xn_ref[...]
    h = _modulated(xw_ref[...], g_ref[...], sc_ref[row:row + 1, :], sh_ref[row:row + 1, :])
    y = jnp.dot(h.astype(bf16), w1_ref[...], preferred_element_type=f32) + b1_ref[...]
    win_ref[...] = y[:, :D_MODEL] * jax.nn.sigmoid(y[:, D_MODEL:])

    @pl.when(i == 0)
    def _():
        win_ref[0:CONV_HALO, :] = jnp.zeros((CONV_HALO, D_MODEL), f32)

    @pl.when(i == n_tiles - 1)
    def _():
        win_ref[CONV_HALO + tm:, :] = jnp.zeros((CONV_HALO, D_MODEL), f32)

    off = CONV_HALO - CONV_WIDTH // 2

    def body(r, carry):
        base = pl.multiple_of(r * rb, rb)
        for lb in range(D_MODEL // LANES):
            ls = slice(lb * LANES, (lb + 1) * LANES)
            win = win_ref[pl.ds(base, rb + 2 * CONV_HALO), ls]
            out = jnp.broadcast_to(dwb_ref[:, ls], (rb, LANES))
            for s in range(8):
                z = None
                for k in range(CONV_WIDTH):
                    if (off + k) % 8 != s:
                        continue
                    j = (off + k) // 8
                    term = win[8 * j:8 * j + rb + 8, :] * dw_ref[k:k + 1, ls]
                    z = term if z is None else z + term
                if z is None:
                    continue
                out = out + (z[0:rb] if s == 0 else pltpu.roll(z, rb + 8 - s, axis=0)[0:rb])
            acc_ref[:, ls] = out
        acc = acc_ref[...]
        mu = jnp.mean(acc, axis=-1, keepdims=True)
        xc = acc - mu
        var = jnp.mean(xc * xc, axis=-1, keepdims=True)
        yn = xc * lax.rsqrt(var + NORM_EPS) * ng_ref[...]
        y_ref[pl.ds(base, rb), :] = _silu(yn).astype(bf16)
        return carry

    lax.fori_loop(0, tm // rb, body, 0)
    o = jnp.dot(y_ref[...], w2_ref[...], preferred_element_type=f32)
    o_ref[...] = xm_ref[...] + gate_ref[row:row + 1, :] * o


def _conv_mixer(x, mods, layer, row, gain, w1, b1, dw, dw_b, norm_g, w2, tm, ffn_w=None):
    t = x.shape[0]
    cast = ffn_w is not None
    c_in, c_out, c_shape = _cast_io(layer) if cast else ([], [], [])
    n_tiles = t // tm
    hb = tm // CONV_HALO
    n_hb = t // CONV_HALO
    return pl.pallas_call(
        functools.partial(_conv_kernel, row, n_tiles, cast),
        out_shape=[jax.ShapeDtypeStruct((t, D_MODEL), f32)] + c_shape,
        grid=(n_tiles,),
        in_specs=[
            pl.BlockSpec((tm, D_MODEL), lambda i: (i, 0)),
            pl.BlockSpec((CONV_HALO, D_MODEL), lambda i: (jnp.maximum(i * hb - 1, 0), 0)),
            pl.BlockSpec((CONV_HALO, D_MODEL), lambda i: (jnp.minimum((i + 1) * hb, n_hb - 1), 0)),
            _resident((1, D_MODEL)),
            _mod_spec(layer, SHIFT_M),
            _mod_spec(layer, SCALE_M),
            _resident((D_MODEL, 2 * D_MODEL)),
            _resident((1, 2 * D_MODEL)),
            _resident((CONV_WIDTH, D_MODEL)),
            _resident((1, D_MODEL)),
            _resident((1, D_MODEL)),
            _resident((D_MODEL, D_MODEL)),
            _mod_spec(layer, GATE_M),
        ] + c_in,
        out_specs=[pl.BlockSpec((tm, D_MODEL), lambda i: (i, 0))] + c_out,
        scratch_shapes=[
            pltpu.VMEM((tm + 2 * CONV_HALO, D_MODEL), f32),
            pltpu.VMEM((tm + 2 * CONV_HALO, D_MODEL), f32),
            pltpu.VMEM((CONV_ROWS, D_MODEL), f32),
            pltpu.VMEM((tm, D_MODEL), bf16),
        ],
        compiler_params=_params("arbitrary"),
        name="conv_mixer",
    )(x, x, x, gain, mods, mods, w1, b1, dw, dw_b, norm_g, w2, mods, *(ffn_w or ()))


def _ret_in_kernel(row, x_ref, g_ref, sh_ref, sc_ref, w_ref, rcos_ref, ccos_ref, rsin_ref, csin_ref,
                   q_ref, k_ref, v_ref, sg_ref):
    h = _modulated(x_ref[...], g_ref[...], sc_ref[row:row + 1, :], sh_ref[row:row + 1, :]).astype(bf16)
    hk = RET_HEADS * RET_QK_DIM
    hv = RET_HEADS * RET_V_DIM
    cos_t = _tile_table(rcos_ref, ccos_ref)
    sin_t = _tile_table(rsin_ref, csin_ref)

    def rope(y, scale):
        outs = []
        for b in range(hk // LANES):
            yb = y[:, b * LANES:(b + 1) * LANES]
            tb = (b % 2) * LANES
            rot = yb * cos_t[:, tb:tb + LANES] + pltpu.roll(yb, LANES // 2, axis=1) * sin_t[:, tb:tb + LANES]
            outs.append(rot * scale)
        return jnp.concatenate(outs, axis=1)

    q = jnp.dot(h, w_ref[:, 0:hk], preferred_element_type=f32)
    q_ref[...] = rope(q, 1.0).astype(bf16)
    k = jnp.dot(h, w_ref[:, hk:2 * hk], preferred_element_type=f32)
    k_ref[...] = rope(k, RET_QK_DIM ** -0.5).astype(bf16)
    v_ref[...] = jnp.dot(h, w_ref[:, 2 * hk:2 * hk + hv], preferred_element_type=f32).astype(bf16)
    g = jnp.dot(h, w_ref[:, 2 * hk + hv:], preferred_element_type=f32)
    sg_ref[...] = _silu(g).astype(bf16)


def _ret_in(x, mods, layer, row, gain, w_in, tabs, tm):
    t = x.shape[0]
    hk = RET_HEADS * RET_QK_DIM
    hv = RET_HEADS * RET_V_DIM
    tok = lambda n: pl.BlockSpec((tm, n), lambda i: (i, 0))
    return pl.pallas_call(
        functools.partial(_ret_in_kernel, row),
        out_shape=[jax.ShapeDtypeStruct((t, hk), bf16), jax.ShapeDtypeStruct((t, hk), bf16),
                   jax.ShapeDtypeStruct((t, hv), bf16), jax.ShapeDtypeStruct((t, hv), bf16)],
        grid=(t // tm,),
        in_specs=[
            tok(D_MODEL),
            _resident((1, D_MODEL)),
            _mod_spec(layer, SHIFT_M),
            _mod_spec(layer, SCALE_M),
            _resident((D_MODEL, 2 * hk + 2 * hv)),
            *_table_specs(tm, RET_QK_DIM),
        ],
        out_specs=[tok(hk), tok(hk), tok(hv), tok(hv)],
        compiler_params=_params("arbitrary"),
        name="ret_in",
    )(x, gain, mods, mods, w_in, *tabs)


def _log_sigmoid(v):
    return jnp.minimum(v, 0.0) - jnp.log1p(jnp.exp(-jnp.abs(v)))


def _ret_state_kernel(k_ref, v_ref, decb_ref, s0b_ref, sball_ref, sb_ref):
    c = RET_CHUNK

    @pl.when(pl.program_id(0) == 0)
    def _():
        sb_ref[...] = s0b_ref[...]

    idx = lax.broadcasted_iota(jnp.int32, (c, 1), 0).astype(f32)
    for h in range(RET_HEADS):
        lg_b = _log_sigmoid(decb_ref[h, 0:1, :])
        kd = jnp.exp(lg_b[:, 0:1] * idx)
        kh = (k_ref[:, h * RET_QK_DIM:(h + 1) * RET_QK_DIM].astype(f32) * kd).astype(bf16)
        a = lax.dot_general(kh, v_ref[:, h * RET_V_DIM:(h + 1) * RET_V_DIM], (((0,), (0,)), ((), ())),
                            preferred_element_type=f32)
        s = sb_ref[h]
        sball_ref[h] = s.astype(bf16)
        sb_ref[h] = s * jnp.exp(lg_b * c) + a


def _ret_state(k, v, dec_b, s0b):
    t = k.shape[0]
    n = t // RET_CHUNK
    st = (RET_HEADS, RET_QK_DIM, RET_V_DIM)
    bwd = lambda w: pl.BlockSpec((RET_CHUNK, w), lambda i: (n - 1 - i, 0))
    return pl.pallas_call(
        _ret_state_kernel,
        out_shape=[jax.ShapeDtypeStruct((n,) + st, bf16), jax.ShapeDtypeStruct(st, f32)],
        grid=(n,),
        in_specs=[bwd(RET_HEADS * RET_QK_DIM), bwd(RET_HEADS * RET_V_DIM),
                  _resident((RET_HEADS, 8, RET_V_DIM)), _resident(st)],
        out_specs=[pl.BlockSpec((None,) + st, lambda i: (n - 1 - i, 0, 0, 0)),
                   pl.BlockSpec(st, lambda i: (0, 0, 0))],
        compiler_params=_params("arbitrary"),
        name="ret_state",
    )(k, v, dec_b, s0b)


def _ret_out_kernel(row, cast, *refs):
    ins, (o_ref, sf_ref), (dec_ref, y_ref), cast_refs = _split_refs(refs, 11, 2, cast)
    q_ref, k_ref, v_ref, sg_ref, sb_ref, s0f_ref, decf_ref, decb_ref, w_ref, x_ref, gate_ref = ins
    _cast_weights(cast_refs)
    c = RET_CHUNK

    @pl.when(pl.program_id(0) == 0)
    def _():
        sf_ref[...] = s0f_ref[...]
        t_i = lax.broadcasted_iota(jnp.int32, (c, c), 0)
        m_i = lax.broadcasted_iota(jnp.int32, (c, c), 1)
        rel = (t_i - m_i).astype(f32)
        for h in range(RET_HEADS):
            lg_f = _log_sigmoid(decf_ref[h, 0:1, :c])
            lg_b = _log_sigmoid(decb_ref[h, 0:1, :c])
            d_f = jnp.where(rel >= 0, jnp.exp(lg_f * jnp.maximum(rel, 0.0)), 0.0)
            d_b = jnp.where(rel <= 0, jnp.exp(lg_b * jnp.maximum(-rel, 0.0)), 0.0)
            dec_ref[h] = d_f + d_b

    idx = lax.broadcasted_iota(jnp.int32, (c, 1), 0).astype(f32)
    for h in range(RET_HEADS):
        lg_f = _log_sigmoid(decf_ref[h, 0:1, :])
        lg_b = _log_sigmoid(decb_ref[h, 0:1, :])
        q = q_ref[:, h * RET_QK_DIM:(h + 1) * RET_QK_DIM]
        k = k_ref[:, h * RET_QK_DIM:(h + 1) * RET_QK_DIM]
        v = v_ref[:, h * RET_V_DIM:(h + 1) * RET_V_DIM]
        s_f = sf_ref[h]
        s = lax.dot_general(q, k, (((1,), (1,)), ((), ())), preferred_element_type=f32)
        o = jnp.dot((s * dec_ref[h]).astype(bf16), v, preferred_element_type=f32)
        o = o + jnp.exp(lg_f[:, 0:1] * (idx + 1.0)) * jnp.dot(q, s_f.astype(bf16), preferred_element_type=f32)
        o = o + jnp.exp(lg_b[:, 0:1] * (c - idx)) * jnp.dot(q, sb_ref[h], preferred_element_type=f32)
        o = o * lax.rsqrt(jnp.mean(o * o, axis=-1, keepdims=True) + NORM_EPS)
        sg = sg_ref[:, h * RET_V_DIM:(h + 1) * RET_V_DIM].astype(f32)
        y_ref[:, h * RET_V_DIM:(h + 1) * RET_V_DIM] = (sg * o).astype(bf16)
        kd = (k.astype(f32) * jnp.exp(lg_f[:, 0:1] * (c - 1.0 - idx))).astype(bf16)
        a = lax.dot_general(kd, v, (((0,), (0,)), ((), ())), preferred_element_type=f32)
        sf_ref[h] = s_f * jnp.exp(lg_f * c) + a
    out = jnp.dot(y_ref[...], w_ref[...], preferred_element_type=f32)
    o_ref[...] = x_ref[...] + gate_ref[row:row + 1, :] * out


def _ret_out(q, k, v, sg, sb_all, s0f, dec_f, dec_b, w_out, x, mods, layer, row, ffn_w=None):
    t = x.shape[0]
    cast = ffn_w is not None
    c_in, c_out, c_shape = _cast_io(layer) if cast else ([], [], [])
    n = t // RET_CHUNK
    hk = RET_HEADS * RET_QK_DIM
    hv = RET_HEADS * RET_V_DIM
    st = (RET_HEADS, RET_QK_DIM, RET_V_DIM)
    tok = lambda w: pl.BlockSpec((RET_CHUNK, w), lambda i: (i, 0))
    return pl.pallas_call(
        functools.partial(_ret_out_kernel, row, cast),
        out_shape=[jax.ShapeDtypeStruct((t, D_MODEL), f32), jax.ShapeDtypeStruct(st, f32)] + c_shape,
        grid=(n,),
        in_specs=[tok(hk), tok(hk), tok(hv), tok(hv),
                  pl.BlockSpec((None,) + st, lambda i: (i, 0, 0, 0)),
                  _resident(st),
                  _resident((RET_HEADS, 8, RET_V_DIM)), _resident((RET_HEADS, 8, RET_V_DIM)),
                  _resident((hv, D_MODEL)),
                  tok(D_MODEL),
                  _mod_spec(layer, GATE_M)] + c_in,
        out_specs=[tok(D_MODEL), pl.BlockSpec(st, lambda i: (0, 0, 0))] + c_out,
        scratch_shapes=[pltpu.VMEM((RET_HEADS, RET_CHUNK, RET_CHUNK), f32),
                        pltpu.VMEM((RET_CHUNK, hv), bf16)],
        compiler_params=_params("arbitrary"),
        name="ret_out",
    )(q, k, v, sg, sb_all, s0f, dec_f, dec_b, w_out, x, mods, *(ffn_w or ()))


def _tile_table_t(row_ref, col_ref):
    rt, ct = row_ref[...], col_ref[...]
    return jnp.concatenate([rt[:, r:r + 1] + ct for r in range(rt.shape[1])], axis=1)


def _att_in_kernel(row, x_ref, g_ref, sh_ref, sc_ref, w_ref, qg_ref, qgp_ref, kg_ref, kgp_ref,
                   rcos_ref, ccos_ref, rsin_ref, csin_ref, qt_ref, k_ref, vt_ref):
    h = _modulated(x_ref[...], g_ref[...], sc_ref[row:row + 1, :], sh_ref[row:row + 1, :]).astype(bf16)
    hd = ATT_HEAD_DIM
    q4 = hd // 4
    nqd = ATT_Q_HEADS * hd
    nkd = ATT_KV_HEADS * hd

    yt = lax.dot_general(w_ref[...], h, (((1,), (1,)), ((), ())), preferred_element_type=f32)
    vt_ref[...] = yt[nqd + nkd:, :].astype(bf16)

    def partner_rows(t):
        return jnp.concatenate([t[q4:2 * q4], t[0:q4], t[3 * q4:], t[2 * q4:3 * q4]], axis=0)

    cos_t = _tile_table_t(rcos_ref, ccos_ref)
    sin_t = _tile_table_t(rsin_ref, csin_ref)

    def norm_rope(y, cos_g, sin_g):
        ms = jnp.sum(y * y, axis=0, keepdims=True) * (1.0 / hd)
        return (y * cos_g + partner_rows(y) * sin_g) * lax.rsqrt(ms + NORM_EPS)

    scale = hd ** -0.5 * LOG2_E
    cos_q, sin_q = cos_t * (qg_ref[...] * scale), sin_t * (qgp_ref[...] * scale)
    for hh in range(ATT_Q_HEADS):
        qt_ref[hh * hd:(hh + 1) * hd, :] = norm_rope(yt[hh * hd:(hh + 1) * hd, :], cos_q, sin_q).astype(bf16)
    cos_k, sin_k = cos_t * kg_ref[...], sin_t * kgp_ref[...]
    k_t = jnp.concatenate([norm_rope(yt[nqd + kh * hd:nqd + (kh + 1) * hd, :], cos_k, sin_k)
                           for kh in range(ATT_KV_HEADS)], axis=0)
    k_ref[...] = k_t.T.astype(bf16)


def _att_in(x, mods, layer, row, gain, w_t, gains, tabs_t, tm):
    t = x.shape[0]
    nqd = ATT_Q_HEADS * ATT_HEAD_DIM
    nkd = ATT_KV_HEADS * ATT_HEAD_DIM
    gr = tm // GRID_W
    row_t = pl.BlockSpec((None, ATT_HEAD_DIM, gr), lambda i: (i, 0, 0))
    col_t = _resident((ATT_HEAD_DIM, GRID_W))
    gain_col = _resident((ATT_HEAD_DIM, 1))
    return pl.pallas_call(
        functools.partial(_att_in_kernel, row),
        out_shape=[jax.ShapeDtypeStruct((nqd, t), bf16), jax.ShapeDtypeStruct((t, nkd), bf16),
                   jax.ShapeDtypeStruct((nkd, t), bf16)],
        grid=(t // tm,),
        in_specs=[
            pl.BlockSpec((tm, D_MODEL), lambda i: (i, 0)),
            _resident((1, D_MODEL)),
            _mod_spec(layer, SHIFT_M),
            _mod_spec(layer, SCALE_M),
            _resident((nqd + 2 * nkd, D_MODEL)),
            gain_col, gain_col, gain_col, gain_col,
            row_t, col_t, row_t, col_t,
        ],
        out_specs=[pl.BlockSpec((nqd, tm), lambda i: (0, i)), pl.BlockSpec((tm, nkd), lambda i: (i, 0)),
                   pl.BlockSpec((nkd, tm), lambda i: (0, i))],
        compiler_params=_params("arbitrary"),
        name="att_in",
    )(x, gain, mods, mods, w_t, *gains, *tabs_t)


def _att_kernel(row, n_blocks, band, cast, *refs):
    ins, (o_ref,), scratch, cast_refs = _split_refs(refs, 13 if band else 7, 1, cast)
    s_refs, e_refs, yt_ref = scratch[:ATT_KV_HEADS], scratch[ATT_KV_HEADS:2 * ATT_KV_HEADS], scratch[-1]
    if band:
        sink_ref, qt_ref, kc_ref, vtc_ref, kp_ref, kn_ref, kx_ref, vtp_ref, vtn_ref, vtx_ref, w_ref, x_ref, gate_ref = ins
    else:
        sink_ref, qt_ref, kc_ref, vtc_ref, w_ref, x_ref, gate_ref = ins
    _cast_weights(cast_refs)
    i = pl.program_id(0)
    c = ATT_BLOCK
    g = ATT_GROUP
    hd = ATT_HEAD_DIM
    n_ctx = kc_ref.shape[0]
    if band:
        key = lax.broadcasted_iota(jnp.int32, (c, g * c), 0)
        qry = lax.broadcasted_iota(jnp.int32, (c, g * c), 1) % c
        prev_ok = (key >= qry + (c - ATT_WINDOW)) & (i > 0)
        next_ok = (key <= qry + (ATT_WINDOW - c)) & (i < n_blocks - 1)
    vals_t = []
    for kh in range(ATT_KV_HEADS):
        sl = slice((kh // 2) * LANES, (kh // 2 + 1) * LANES)
        vs = slice(kh * hd, (kh + 1) * hd)
        q_t = jnp.concatenate([qt_ref[(kh * g + gg) * hd:(kh * g + gg + 1) * hd, :] for gg in range(g)], axis=1)
        pad = jnp.zeros_like(q_t)
        q_t = jnp.concatenate([q_t, pad] if kh % 2 == 0 else [pad, q_t], axis=0)
        if band:
            keys = jnp.concatenate([kc_ref[:, sl], kp_ref[:, sl], kx_ref[:, sl], kn_ref[:, sl]], axis=0)
            vals_t.append(jnp.concatenate([vtc_ref[vs, :], vtp_ref[vs, :], vtx_ref[vs, :], vtn_ref[vs, :]], axis=1))
        else:
            keys = kc_ref[:, sl]
            vals_t.append(vtc_ref[vs, :])
        s_refs[kh][...] = jnp.dot(keys, q_t, preferred_element_type=f32)
    for kh in range(ATT_KV_HEADS):
        s = s_refs[kh][...]
        if band:
            s = jnp.concatenate([
                s[:n_ctx],
                jnp.where(prev_ok, s[n_ctx:n_ctx + c], NEG_INF),
                s[n_ctx + c:n_ctx + 2 * c],
                jnp.where(next_ok, s[n_ctx + 2 * c:], NEG_INF)], axis=0)
        sink = jnp.concatenate([jnp.full((1, c), sink_ref[kh * g + gg] * LOG2_E, f32) for gg in range(g)], axis=1)
        m = jnp.maximum(jnp.max(s, axis=0, keepdims=True), sink)
        e = jnp.exp2(s - m)
        denom = jnp.sum(e, axis=0, keepdims=True) + jnp.exp2(sink - m)
        e_refs[kh][...] = e.astype(bf16)
        o_t = jnp.dot(vals_t[kh], e_refs[kh][...], preferred_element_type=f32) * (1.0 / denom)
        for gg in range(g):
            h = kh * g + gg
            yt_ref[h * hd:(h + 1) * hd, :] = o_t[:, gg * c:(gg + 1) * c].astype(bf16)
    out = lax.dot_general(yt_ref[...], w_ref[...], (((0,), (0,)), ((), ())), preferred_element_type=f32)
    o_ref[...] = x_ref[...] + gate_ref[row:row + 1, :] * out


def _attention(sink, q_t, kc, vtc, kx, vtx, w_o, x, mods, layer, row, ffn_w=None):
    t = x.shape[0]
    cast = ffn_w is not None
    c_in, c_out, c_shape = _cast_io(layer) if cast else ([], [], [])
    c = ATT_BLOCK
    n = t // c
    nqd = ATT_Q_HEADS * ATT_HEAD_DIM
    nk = nv = ATT_KV_HEADS * ATT_HEAD_DIM
    band = kx is not None
    n_keys = kc.shape[0] + (3 * c if band else 0)
    tok = lambda w: pl.BlockSpec((c, w), lambda i: (i, 0))
    before = lambda i: jnp.maximum(i - 1, 0)
    after = lambda i: jnp.minimum(i + 1, n - 1)
    in_specs = [pl.BlockSpec(memory_space=pltpu.SMEM), pl.BlockSpec((nqd, c), lambda i: (0, i)),
                _resident(kc.shape), _resident(vtc.shape)]
    args = [sink, q_t, kc, vtc]
    if band:
        in_specs += [pl.BlockSpec((c, nk), lambda i: (before(i), 0)), pl.BlockSpec((c, nk), lambda i: (after(i), 0)),
                     tok(nk),
                     pl.BlockSpec((nv, c), lambda i: (0, before(i))), pl.BlockSpec((nv, c), lambda i: (0, after(i))),
                     pl.BlockSpec((nv, c), lambda i: (0, i))]
        args += [kx, kx, kx, vtx, vtx, vtx]
    in_specs += [_resident((nqd, D_MODEL)), tok(D_MODEL), _mod_spec(layer, GATE_M)] + c_in
    args += [w_o, x, mods, *(ffn_w or ())]
    return pl.pallas_call(
        functools.partial(_att_kernel, row, n, band, cast),
        out_shape=[jax.ShapeDtypeStruct((t, D_MODEL), f32)] + c_shape,
        grid=(n,),
        in_specs=in_specs,
        out_specs=[tok(D_MODEL)] + c_out,
        scratch_shapes=([pltpu.VMEM((n_keys, ATT_GROUP * c), f32)] * ATT_KV_HEADS
                        + [pltpu.VMEM((n_keys, ATT_GROUP * c), bf16)] * ATT_KV_HEADS
                        + [pltpu.VMEM((nqd, c), bf16)]),
        compiler_params=_params("arbitrary"),
        name="att_band" if band else "att_ctx",
    )(*args)


def _ffn_kernel(row, x_ref, g_ref, sh_ref, sc_ref, gate_ref, wgu_ref, wd_ref, o_ref):
    x = x_ref[...]
    h = _modulated(x, g_ref[...], sc_ref[row:row + 1, :], sh_ref[row:row + 1, :]).astype(bf16)
    acc = jnp.zeros(x.shape, f32)
    for c0 in range(0, FFN_HIDDEN, FFN_CHUNK):
        a = jnp.dot(h, wgu_ref[:, c0:c0 + FFN_CHUNK], preferred_element_type=f32)
        b = jnp.dot(h, wgu_ref[:, FFN_HIDDEN + c0:FFN_HIDDEN + c0 + FFN_CHUNK], preferred_element_type=f32)
        act = (_silu(a) * b).astype(bf16)
        acc = acc + jnp.dot(act, wd_ref[c0:c0 + FFN_CHUNK, :], preferred_element_type=f32)
    o_ref[...] = x + gate_ref[row:row + 1, :] * acc


def _ffn(x, mods, layer, row, gain, w_gu, w_down, tm):
    t = x.shape[0]
    tok = pl.BlockSpec((tm, D_MODEL), lambda i: (i, 0))
    return pl.pallas_call(
        functools.partial(_ffn_kernel, row),
        out_shape=jax.ShapeDtypeStruct((t, D_MODEL), f32),
        grid=(t // tm,),
        in_specs=[tok, _resident((1, D_MODEL)),
                  _mod_spec(layer, SHIFT_F), _mod_spec(layer, SCALE_F), _mod_spec(layer, GATE_F),
                  _resident((D_MODEL, 2 * FFN_HIDDEN)), _resident((FFN_HIDDEN, D_MODEL))],
        out_specs=tok,
        compiler_params=_params("arbitrary"),
        name="ffn",
    )(x, gain, mods, mods, mods, w_gu, w_down)


def _rope_tables(t, quarter, layout):
    n_rows = t // GRID_W
    inv = ROPE_BASE ** (-jnp.arange(quarter, dtype=f32) / quarter)
    ang = {'r': jnp.arange(n_rows).astype(f32)[:, None] * inv, 'c': jnp.arange(GRID_W).astype(f32)[:, None] * inv}
    n = {'r': n_rows, 'c': GRID_W}

    def table(axis, fn, signed):
        parts = []
        for grp in layout:
            if isinstance(grp, int):
                parts.append(jnp.zeros((n[axis], grp), f32))
            elif grp[0] == axis:
                sign = -1.0 if (signed and grp[1] == '1') else 1.0
                parts.append(sign * fn(ang[axis]))
            else:
                parts.append(jnp.zeros((n[axis], quarter), f32))
        return jnp.concatenate(parts, axis=1)

    return (table('r', jnp.cos, False), table('c', jnp.cos, False),
            table('r', jnp.sin, True), table('c', jnp.sin, True))


RET_ROPE_LAYOUT = ('r1', 'r2', 'c1', 'c2')
ATT_ROPE_LAYOUT = ('r1', 'r2', 'c1', 'c2')


def _transposed_tables(tabs, tm):
    gr = tm // GRID_W
    row_t = lambda a: a.T.reshape(a.shape[1], a.shape[0] // gr, gr).transpose(1, 0, 2)
    return row_t(tabs[0]), tabs[1].T, row_t(tabs[2]), tabs[3].T


def _identity_tables(t, width):
    n_rows = t // GRID_W
    return (jnp.ones((n_rows, width), f32), jnp.zeros((GRID_W, width), f32),
            jnp.zeros((n_rows, width), f32), jnp.zeros((GRID_W, width), f32))


def kernel(x, c, ctx, c_ctx, ada_w, ada_b, norm_mix, norm_ffn, conv_w1, conv_b1, conv_dw, conv_dw_b, conv_norm, conv_w2, ret_w_in, ret_decay_f, ret_decay_b, ret_w_out, att_w_qkv, att_q_norm, att_k_norm, att_sink, att_w_o, ffn_w_gu, ffn_w_down):
    assert x.shape[0] == 1 and c.shape[0] == 1 and ctx.shape[0] == 1
    t_lat, t_ctx = x.shape[1], ctx.shape[1]
    tm_lat, tm_ctx = 512, t_ctx
    xs, hc = x[0], ctx[0]
    lat, cx = 0, 1

    cond8 = jnp.zeros((8, D_MODEL), f32).at[lat].set(c[0]).at[cx].set(c_ctx)
    mods = _adaln(cond8, ada_w, ada_b)

    row1 = lambda v: v.reshape(1, -1)
    ffn_w = (ffn_w_gu, ffn_w_down)
    for i in range(DEPTH):
        kind, j, last = i % N_MIXERS, i // N_MIXERS, i == DEPTH - 1
        with_ctx = not last
        g_mix = row1(norm_mix[i])
        if kind == 0:
            w1, w2 = conv_w1[j].astype(bf16), conv_w2[j].astype(bf16)
            cargs = (g_mix, w1, row1(conv_b1[j]), conv_dw[j], row1(conv_dw_b[j]), row1(conv_norm[j]), w2)
            xs, w_gu, w_down = _conv_mixer(xs, mods, i, lat, *cargs, tm_lat, ffn_w=ffn_w)
            if with_ctx:
                hc, = _conv_mixer(hc, mods, i, cx, *cargs, tm_ctx)
        elif kind == 1:
            w_in, w_out = ret_w_in[j].astype(bf16), ret_w_out[j].astype(bf16)
            bdec = lambda d: jnp.broadcast_to(d[:, None, None], (RET_HEADS, 8, RET_V_DIM)).astype(f32)
            dec_f, dec_b = bdec(ret_decay_f[j]), bdec(ret_decay_b[j])
            zeros = jnp.zeros((RET_HEADS, RET_QK_DIM, RET_V_DIM), f32)
            qc, kc, vc, gc = _ret_in(hc, mods, i, cx, g_mix, w_in, _identity_tables(t_ctx, RET_QK_DIM), tm_ctx)
            sb_c, s0b = _ret_state(kc, vc, dec_b, zeros)
            hc_new, s0f = _ret_out(qc, kc, vc, gc, sb_c, zeros, dec_f, dec_b, w_out, hc, mods, i, cx)
            qx, kx, vx, gx = _ret_in(xs, mods, i, lat, g_mix, w_in, _rope_tables(t_lat, RET_QK_DIM // 4, RET_ROPE_LAYOUT), tm_lat)
            sb_x, _ = _ret_state(kx, vx, dec_b, s0b)
            xs, _, w_gu, w_down = _ret_out(qx, kx, vx, gx, sb_x, s0f, dec_f, dec_b, w_out, xs, mods, i, lat, ffn_w=ffn_w)
            if with_ctx:
                hc = hc_new
        else:
            w_t = att_w_qkv[j].T.astype(bf16)
            w_o = att_w_o[j].astype(bf16)
            q4 = ATT_HEAD_DIM // 4
            partner = lambda v: jnp.concatenate([v[q4:2 * q4], v[:q4], v[3 * q4:], v[2 * q4:3 * q4]])
            col = lambda v: v.astype(f32).reshape(-1, 1)
            gains = (col(att_q_norm[j]), col(partner(att_q_norm[j])), col(att_k_norm[j]), col(partner(att_k_norm[j])))
            sink = att_sink[j].astype(f32)
            tabs_c = _transposed_tables(_identity_tables(t_ctx, ATT_HEAD_DIM), tm_ctx)
            tabs_x = _transposed_tables(_rope_tables(t_lat, q4, ATT_ROPE_LAYOUT), tm_lat)
            qc, kc, vtc = _att_in(hc, mods, i, cx, g_mix, w_t, gains, tabs_c, tm_ctx)
            qx, kx, vtx = _att_in(xs, mods, i, lat, g_mix, w_t, gains, tabs_x, tm_lat)
            xs, w_gu, w_down = _attention(sink, qx, kc, vtc, kx, vtx, w_o, xs, mods, i, lat, ffn_w=ffn_w)
            if with_ctx:
                hc, = _attention(sink, qc, kc, vtc, None, None, w_o, hc, mods, i, cx)
        g_ffn = row1(norm_ffn[i])
        xs = _ffn(xs, mods, i, lat, g_ffn, w_gu, w_down, tm_lat)
        if with_ctx:
            hc = _ffn(hc, mods, i, cx, g_ffn, w_gu, w_down, tm_ctx)
    return xs[None]
```

```python
import functools

import jax
import jax.numpy as jnp
from jax import lax
from jax.experimental import pallas as pl
from jax.experimental.pallas import tpu as pltpu

f32 = jnp.float32
bf16 = jnp.bfloat16

D_MODEL = 1024
DEPTH = 4
N_COND = 2
GRID_W = 64
N_MIXERS = 3
CONV_WIDTH = 31
CONV_HALO = 16
CONV_ROWS = 128
RET_HEADS = 4
RET_QK_DIM = 256
RET_V_DIM = 512
RET_CHUNK = 256
ATT_Q_HEADS = 16
ATT_KV_HEADS = 4
ATT_GROUP = ATT_Q_HEADS // ATT_KV_HEADS
ATT_HEAD_DIM = 64
ATT_WINDOW = 128
ATT_BLOCK = 128
FFN_HIDDEN = 2816
FFN_CHUNK = 256
CAST_GU_STEPS = 32
CAST_DN_STEPS = 16
ROPE_BASE = 10000.0
NORM_EPS = 1e-6
NEG_INF = -1e30
LOG2_E = 1.4426950408889634
LANES = 128
VMEM_LIMIT = 56 * 1024 * 1024

SHIFT_M, SCALE_M, GATE_M, SHIFT_F, SCALE_F, GATE_F = range(6)


def _params(*sem):
    return pltpu.CompilerParams(dimension_semantics=sem, vmem_limit_bytes=VMEM_LIMIT)


def _resident(shape):
    nd = len(shape)
    return pl.BlockSpec(shape, lambda *_: (0,) * nd, pipeline_mode=pl.Buffered(1))


def _mod_spec(layer, which):
    return pl.BlockSpec((None, 8, D_MODEL), lambda *_: (layer, 0, which))


def _modulated(x, gain, scale, shift):
    ms = jnp.mean(x * x, axis=-1, keepdims=True)
    return (x * lax.rsqrt(ms + NORM_EPS)) * (gain * (1.0 + scale)) + shift


def _silu(v):
    return v * jax.nn.sigmoid(v)


def _tile_table(row_ref, col_ref):
    return jnp.concatenate([row_ref[r:r + 1, :] + col_ref[...] for r in range(row_ref.shape[0])], axis=0)


def _table_specs(tm, width):
    row = pl.BlockSpec((tm // GRID_W, width), lambda i: (i, 0))
    col = _resident((GRID_W, width))
    return [row, col, row, col]


def _cast_io(layer):
    gu_rows = D_MODEL // CAST_GU_STEPS
    dn_rows = FFN_HIDDEN // CAST_DN_STEPS
    gu_i = lambda i: jnp.minimum(i, CAST_GU_STEPS - 1)
    dn_i = lambda i: jnp.minimum(i, CAST_DN_STEPS - 1)
    in_specs = [pl.BlockSpec((None, gu_rows, 2 * FFN_HIDDEN), lambda i: (layer, gu_i(i), 0)),
                pl.BlockSpec((None, dn_rows, D_MODEL), lambda i: (layer, dn_i(i), 0))]
    out_specs = [pl.BlockSpec((gu_rows, 2 * FFN_HIDDEN), lambda i: (gu_i(i), 0)),
                 pl.BlockSpec((dn_rows, D_MODEL), lambda i: (dn_i(i), 0))]
    out_shape = [jax.ShapeDtypeStruct((D_MODEL, 2 * FFN_HIDDEN), bf16),
                 jax.ShapeDtypeStruct((FFN_HIDDEN, D_MODEL), bf16)]
    return in_specs, out_specs, out_shape


def _cast_weights(cast_refs):
    if not cast_refs:
        return
    gu_ref, dn_ref, gu_out_ref, dn_out_ref = cast_refs
    i = pl.program_id(0)

    @pl.when(i < CAST_GU_STEPS)
    def _():
        gu_out_ref[...] = gu_ref[...].astype(bf16)

    @pl.when(i < CAST_DN_STEPS)
    def _():
        dn_out_ref[...] = dn_ref[...].astype(bf16)


def _split_refs(refs, n_in, n_out, cast):
    extra = 2 if cast else 0
    ins = refs[:n_in]
    outs = refs[n_in + extra:n_in + extra + n_out]
    scratch = refs[n_in + 2 * extra + n_out:]
    cast_refs = refs[n_in:n_in + extra] + refs[n_in + extra + n_out:n_in + 2 * extra + n_out]
    return ins, outs, scratch, cast_refs


def _adaln_kernel(ct_ref, w_ref, b_ref, o_ref):
    s = _silu(ct_ref[...])
    w = w_ref[...]
    rows = [jnp.sum(w * s[:, r:r + 1], axis=0, keepdims=True) for r in range(N_COND)]
    rows.append(jnp.zeros((8 - N_COND, w.shape[1]), f32))
    o_ref[...] = jnp.concatenate(rows, axis=0) + b_ref[...]


def _adaln(cond_t, ada_w, ada_b):
    tn = 1536
    n = 6 * D_MODEL
    return pl.pallas_call(
        _adaln_kernel,
        out_shape=jax.ShapeDtypeStruct((DEPTH, 8, n), f32),
        grid=(DEPTH, n // tn),
        in_specs=[
            pl.BlockSpec((D_MODEL, 8), lambda l, j: (0, 0)),
            pl.BlockSpec((None, D_MODEL, tn), lambda l, j: (l, 0, j)),
            pl.BlockSpec((None, 1, tn), lambda l, j: (l, 0, j)),
        ],
        out_specs=pl.BlockSpec((None, 8, tn), lambda l, j: (l, 0, j)),
        compiler_params=_params("arbitrary", "arbitrary"),
        name="adaln",
    )(cond_t, ada_w, ada_b.reshape(DEPTH, 1, n))


def _conv_kernel(row, n_tiles, cast, *refs):
    ins, (o_ref,), (xw_ref, win_ref, acc_ref, y_ref), cast_refs = _split_refs(refs, 13, 1, cast)
    xm_ref, xp_ref, xn_ref, g_ref, sh_ref, sc_ref, w1_ref, b1_ref, dw_ref, dwb_ref, ng_ref, w2_ref, gate_ref = ins
    _cast_weights(cast_refs)
    i = pl.program_id(0)
    tm = xm_ref.shape[0]
    rb = CONV_ROWS
    xw_ref[0:CONV_HALO, :] = xp_ref[...]
    xw_ref[CONV_HALO:CONV_HALO + tm, :] = xm_ref[...]
    xw_ref[CONV_HALO + tm:, :] = xn_ref[...]
    h = _modulated(xw_ref[...], g_ref[...], sc_ref[row:row + 1, :], sh_ref[row:row + 1, :])
    y = jnp.dot(h.astype(bf16), w1_ref[...], preferred_element_type=f32) + b1_ref[...]
    win_ref[...] = y[:, :D_MODEL] * jax.nn.sigmoid(y[:, D_MODEL:])

    @pl.when(i == 0)
    def _():
        win_ref[0:CONV_HALO, :] = jnp.zeros((CONV_HALO, D_MODEL), f32)

    @pl.when(i == n_tiles - 1)
    def _():
        win_ref[CONV_HALO + tm:, :] = jnp.zeros((CONV_HALO, D_MODEL), f32)

    off = CONV_HALO - CONV_WIDTH // 2

    def body(r, carry):
        base = pl.multiple_of(r * rb, rb)
        for lb in range(D_MODEL // LANES):
            ls = slice(lb * LANES, (lb + 1) * LANES)
            win = win_ref[pl.ds(base, rb + 2 * CONV_HALO), ls]
            out = jnp.broadcast_to(dwb_ref[:, ls], (rb, LANES))
            for s in range(8):
                z = None
                for k in range(CONV_WIDTH):
                    if (off + k) % 8 != s:
                        continue
                    j = (off + k) // 8
                    term = win[8 * j:8 * j + rb + 8, :] * dw_ref[k:k + 1, ls]
                    z = term if z is None else z + term
                if z is None:
                    continue
                out = out + (z[0:rb] if s == 0 else pltpu.roll(z, rb + 8 - s, axis=0)[0:rb])
            acc_ref[:, ls] = out
        acc = acc_ref[...]
        mu = jnp.mean(acc, axis=-1, keepdims=True)
        xc = acc - mu
        var = jnp.mean(xc * xc, axis=-1, keepdims=True)
        yn = xc * lax.rsqrt(var + NORM_EPS) * ng_ref[...]
        y_ref[pl.ds(base, rb), :] = _silu(yn).astype(bf16)
        return carry

    lax.fori_loop(0, tm // rb, body, 0)
    o = jnp.dot(y_ref[...], w2_ref[...], preferred_element_type=f32)
    o_ref[...] = xm_ref[...] + gate_ref[row:row + 1, :] * o


def _conv_mixer(x, mods, layer, row, gain, w1, b1, dw, dw_b, norm_g, w2, tm, ffn_w=None):
    t = x.shape[0]
    cast = ffn_w is not None
    c_in, c_out, c_shape = _cast_io(layer) if cast else ([], [], [])
    n_tiles = t // tm
    hb = tm // CONV_HALO
    n_hb = t // CONV_HALO
    return pl.pallas_call(
        functools.partial(_conv_kernel, row, n_tiles, cast),
        out_shape=[jax.ShapeDtypeStruct((t, D_MODEL), f32)] + c_shape,
        grid=(n_tiles,),
        in_specs=[
            pl.BlockSpec((tm, D_MODEL), lambda i: (i, 0)),
            pl.BlockSpec((CONV_HALO, D_MODEL), lambda i: (jnp.maximum(i * hb - 1, 0), 0)),
            pl.BlockSpec((CONV_HALO, D_MODEL), lambda i: (jnp.minimum((i + 1) * hb, n_hb - 1), 0)),
            _resident((1, D_MODEL)),
            _mod_spec(layer, SHIFT_M),
            _mod_spec(layer, SCALE_M),
            _resident((D_MODEL, 2 * D_MODEL)),
            _resident((1, 2 * D_MODEL)),
            _resident((CONV_WIDTH, D_MODEL)),
            _resident((1, D_MODEL)),
            _resident((1, D_MODEL)),
            _resident((D_MODEL, D_MODEL)),
            _mod_spec(layer, GATE_M),
        ] + c_in,
        out_specs=[pl.BlockSpec((tm, D_MODEL), lambda i: (i, 0))] + c_out,
        scratch_shapes=[
            pltpu.VMEM((tm + 2 * CONV_HALO, D_MODEL), f32),
            pltpu.VMEM((tm + 2 * CONV_HALO, D_MODEL), f32),
            pltpu.VMEM((CONV_ROWS, D_MODEL), f32),
            pltpu.VMEM((tm, D_MODEL), bf16),
        ],
        compiler_params=_params("arbitrary"),
        name="conv_mixer",
    )(x, x, x, gain, mods, mods, w1, b1, dw, dw_b, norm_g, w2, mods, *(ffn_w or ()))


def _log_sigmoid(v):
    return jnp.minimum(v, 0.0) - jnp.log1p(jnp.exp(-jnp.abs(v)))


def _ret_in_kernel(row, x_ref, g_ref, sh_ref, sc_ref, w_ref, rcos_ref, ccos_ref, rsin_ref, csin_ref,
                   decb_ref, s0b_ref, q_ref, k_ref, v_ref, sg_ref, sball_ref, sb_ref):
    h = _modulated(x_ref[...], g_ref[...], sc_ref[row:row + 1, :], sh_ref[row:row + 1, :]).astype(bf16)
    hk = RET_HEADS * RET_QK_DIM
    hv = RET_HEADS * RET_V_DIM
    cos_t = _tile_table(rcos_ref, ccos_ref)
    sin_t = _tile_table(rsin_ref, csin_ref)

    def rope(y, scale):
        outs = []
        for b in range(hk // LANES):
            yb = y[:, b * LANES:(b + 1) * LANES]
            tb = (b % 2) * LANES
            rot = yb * cos_t[:, tb:tb + LANES] + pltpu.roll(yb, LANES // 2, axis=1) * sin_t[:, tb:tb + LANES]
            outs.append(rot * scale)
        return jnp.concatenate(outs, axis=1)

    q = jnp.dot(h, w_ref[:, 0:hk], preferred_element_type=f32)
    q_ref[...] = rope(q, 1.0).astype(bf16)
    k = rope(jnp.dot(h, w_ref[:, hk:2 * hk], preferred_element_type=f32), RET_QK_DIM ** -0.5).astype(bf16)
    k_ref[...] = k
    v = jnp.dot(h, w_ref[:, 2 * hk:2 * hk + hv], preferred_element_type=f32).astype(bf16)
    v_ref[...] = v
    g = jnp.dot(h, w_ref[:, 2 * hk + hv:], preferred_element_type=f32)
    sg_ref[...] = _silu(g).astype(bf16)

    c = RET_CHUNK

    @pl.when(pl.program_id(0) == 0)
    def _():
        sb_ref[...] = s0b_ref[...]

    idx = lax.broadcasted_iota(jnp.int32, (c, 1), 0).astype(f32)
    for ci in reversed(range(x_ref.shape[0] // c)):
        rows = slice(ci * c, (ci + 1) * c)
        for hh in range(RET_HEADS):
            lg_b = _log_sigmoid(decb_ref[hh, 0:1, :])
            kd = jnp.exp(lg_b[:, 0:1] * idx)
            kh = (k[rows, hh * RET_QK_DIM:(hh + 1) * RET_QK_DIM].astype(f32) * kd).astype(bf16)
            a = lax.dot_general(kh, v[rows, hh * RET_V_DIM:(hh + 1) * RET_V_DIM], (((0,), (0,)), ((), ())),
                                preferred_element_type=f32)
            s = sb_ref[hh]
            sball_ref[ci, hh] = s.astype(bf16)
            sb_ref[hh] = s * jnp.exp(lg_b * c) + a


def _ret_in(x, mods, layer, row, gain, w_in, tabs, dec_b, s0b, tm):
    t = x.shape[0]
    n = t // tm
    hk = RET_HEADS * RET_QK_DIM
    hv = RET_HEADS * RET_V_DIM
    st = (RET_HEADS, RET_QK_DIM, RET_V_DIM)
    tok = lambda w: pl.BlockSpec((tm, w), lambda i: (n - 1 - i, 0))
    row_tab = pl.BlockSpec((tm // GRID_W, RET_QK_DIM), lambda i: (n - 1 - i, 0))
    col_tab = _resident((GRID_W, RET_QK_DIM))
    return pl.pallas_call(
        functools.partial(_ret_in_kernel, row),
        out_shape=[jax.ShapeDtypeStruct((t, hk), bf16), jax.ShapeDtypeStruct((t, hk), bf16),
                   jax.ShapeDtypeStruct((t, hv), bf16), jax.ShapeDtypeStruct((t, hv), bf16),
                   jax.ShapeDtypeStruct((t // RET_CHUNK,) + st, bf16), jax.ShapeDtypeStruct(st, f32)],
        grid=(n,),
        in_specs=[
            tok(D_MODEL),
            _resident((1, D_MODEL)),
            _mod_spec(layer, SHIFT_M),
            _mod_spec(layer, SCALE_M),
            _resident((D_MODEL, 2 * hk + 2 * hv)),
            row_tab, col_tab, row_tab, col_tab,
            _resident((RET_HEADS, 8, RET_V_DIM)),
            _resident(st),
        ],
        out_specs=[tok(hk), tok(hk), tok(hv), tok(hv),
                   pl.BlockSpec((tm // RET_CHUNK,) + st, lambda i: (n - 1 - i, 0, 0, 0)),
                   pl.BlockSpec(st, lambda i: (0, 0, 0))],
        compiler_params=_params("arbitrary"),
        name="ret_in",
    )(x, gain, mods, mods, w_in, *tabs, dec_b, s0b)


def _ret_out_kernel(row, cast, *refs):
    ins, (o_ref, sf_ref), (dec_ref, y_ref), cast_refs = _split_refs(refs, 11, 2, cast)
    q_ref, k_ref, v_ref, sg_ref, sb_ref, s0f_ref, decf_ref, decb_ref, w_ref, x_ref, gate_ref = ins
    _cast_weights(cast_refs)
    c = RET_CHUNK

    @pl.when(pl.program_id(0) == 0)
    def _():
        sf_ref[...] = s0f_ref[...]
        t_i = lax.broadcasted_iota(jnp.int32, (c, c), 0)
        m_i = lax.broadcasted_iota(jnp.int32, (c, c), 1)
        rel = (t_i - m_i).astype(f32)
        for h in range(RET_HEADS):
            lg_f = _log_sigmoid(decf_ref[h, 0:1, :c])
            lg_b = _log_sigmoid(decb_ref[h, 0:1, :c])
            d_f = jnp.where(rel >= 0, jnp.exp(lg_f * jnp.maximum(rel, 0.0)), 0.0)
            d_b = jnp.where(rel <= 0, jnp.exp(lg_b * jnp.maximum(-rel, 0.0)), 0.0)
            dec_ref[h] = d_f + d_b

    idx = lax.broadcasted_iota(jnp.int32, (c, 1), 0).astype(f32)
    for h in range(RET_HEADS):
        lg_f = _log_sigmoid(decf_ref[h, 0:1, :])
        lg_b = _log_sigmoid(decb_ref[h, 0:1, :])
        q = q_ref[:, h * RET_QK_DIM:(h + 1) * RET_QK_DIM]
        k = k_ref[:, h * RET_QK_DIM:(h + 1) * RET_QK_DIM]
        v = v_ref[:, h * RET_V_DIM:(h + 1) * RET_V_DIM]
        s_f = sf_ref[h]
        s = lax.dot_general(q, k, (((1,), (1,)), ((), ())), preferred_element_type=f32)
        o = jnp.dot((s * dec_ref[h]).astype(bf16), v, preferred_element_type=f32)
        o = o + jnp.exp(lg_f[:, 0:1] * (idx + 1.0)) * jnp.dot(q, s_f.astype(bf16), preferred_element_type=f32)
        o = o + jnp.exp(lg_b[:, 0:1] * (c - idx)) * jnp.dot(q, sb_ref[h], preferred_element_type=f32)
        o = o * lax.rsqrt(jnp.mean(o * o, axis=-1, keepdims=True) + NORM_EPS)
        sg = sg_ref[:, h * RET_V_DIM:(h + 1) * RET_V_DIM].astype(f32)
        y_ref[:, h * RET_V_DIM:(h + 1) * RET_V_DIM] = (sg * o).astype(bf16)
        kd = (k.astype(f32) * jnp.exp(lg_f[:, 0:1] * (c - 1.0 - idx))).astype(bf16)
        a = lax.dot_general(kd, v, (((0,), (0,)), ((), ())), preferred_element_type=f32)
        sf_ref[h] = s_f * jnp.exp(lg_f * c) + a
    out = jnp.dot(y_ref[...], w_ref[...], preferred_element_type=f32)
    o_ref[...] = x_ref[...] + gate_ref[row:row + 1, :] * out


def _ret_out(q, k, v, sg, sb_all, s0f, dec_f, dec_b, w_out, x, mods, layer, row, ffn_w=None):
    t = x.shape[0]
    cast = ffn_w is not None
    c_in, c_out, c_shape = _cast_io(layer) if cast else ([], [], [])
    n = t // RET_CHUNK
    hk = RET_HEADS * RET_QK_DIM
    hv = RET_HEADS * RET_V_DIM
    st = (RET_HEADS, RET_QK_DIM, RET_V_DIM)
    tok = lambda w: pl.BlockSpec((RET_CHUNK, w), lambda i: (i, 0))
    return pl.pallas_call(
        functools.partial(_ret_out_kernel, row, cast),
        out_shape=[jax.ShapeDtypeStruct((t, D_MODEL), f32), jax.ShapeDtypeStruct(st, f32)] + c_shape,
        grid=(n,),
        in_specs=[tok(hk), tok(hk), tok(hv), tok(hv),
                  pl.BlockSpec((None,) + st, lambda i: (i, 0, 0, 0)),
                  _resident(st),
                  _resident((RET_HEADS, 8, RET_V_DIM)), _resident((RET_HEADS, 8, RET_V_DIM)),
                  _resident((hv, D_MODEL)),
                  tok(D_MODEL),
                  _mod_spec(layer, GATE_M)] + c_in,
        out_specs=[tok(D_MODEL), pl.BlockSpec(st, lambda i: (0, 0, 0))] + c_out,
        scratch_shapes=[pltpu.VMEM((RET_HEADS, RET_CHUNK, RET_CHUNK), f32),
                        pltpu.VMEM((RET_CHUNK, hv), bf16)],
        compiler_params=_params("arbitrary"),
        name="ret_out",
    )(q, k, v, sg, sb_all, s0f, dec_f, dec_b, w_out, x, mods, *(ffn_w or ()))


def _tile_table_t(row_ref, col_ref):
    rt, ct = row_ref[...], col_ref[...]
    return jnp.concatenate([rt[:, r:r + 1] + ct for r in range(rt.shape[1])], axis=1)


def _att_in_kernel(row, x_ref, g_ref, sh_ref, sc_ref, w_ref, qg_ref, qgp_ref, kg_ref, kgp_ref,
                   rcos_ref, ccos_ref, rsin_ref, csin_ref, qt_ref, k_ref, vt_ref):
    h = _modulated(x_ref[...], g_ref[...], sc_ref[row:row + 1, :], sh_ref[row:row + 1, :]).astype(bf16)
    hd = ATT_HEAD_DIM
    q4 = hd // 4
    nqd = ATT_Q_HEADS * hd
    nkd = ATT_KV_HEADS * hd

    yt = lax.dot_general(w_ref[...], h, (((1,), (1,)), ((), ())), preferred_element_type=f32)
    vt_ref[...] = yt[nqd + nkd:, :].astype(bf16)

    def partner_rows(t):
        return jnp.concatenate([t[q4:2 * q4], t[0:q4], t[3 * q4:], t[2 * q4:3 * q4]], axis=0)

    cos_t = _tile_table_t(rcos_ref, ccos_ref)
    sin_t = _tile_table_t(rsin_ref, csin_ref)

    def norm_rope(y, cos_g, sin_g):
        ms = jnp.sum(y * y, axis=0, keepdims=True) * (1.0 / hd)
        return (y * cos_g + partner_rows(y) * sin_g) * lax.rsqrt(ms + NORM_EPS)

    scale = hd ** -0.5 * LOG2_E
    cos_q, sin_q = cos_t * (qg_ref[...] * scale), sin_t * (qgp_ref[...] * scale)
    for hh in range(ATT_Q_HEADS):
        qt_ref[hh * hd:(hh + 1) * hd, :] = norm_rope(yt[hh * hd:(hh + 1) * hd, :], cos_q, sin_q).astype(bf16)
    cos_k, sin_k = cos_t * kg_ref[...], sin_t * kgp_ref[...]
    k_t = jnp.concatenate([norm_rope(yt[nqd + kh * hd:nqd + (kh + 1) * hd, :], cos_k, sin_k)
                           for kh in range(ATT_KV_HEADS)], axis=0)
    k_ref[...] = k_t.T.astype(bf16)


def _att_in(x, mods, layer, row, gain, w_t, gains, tabs_t, tm):
    t = x.shape[0]
    nqd = ATT_Q_HEADS * ATT_HEAD_DIM
    nkd = ATT_KV_HEADS * ATT_HEAD_DIM
    gr = tm // GRID_W
    row_t = pl.BlockSpec((None, ATT_HEAD_DIM, gr), lambda i: (i, 0, 0))
    col_t = _resident((ATT_HEAD_DIM, GRID_W))
    gain_col = _resident((ATT_HEAD_DIM, 1))
    return pl.pallas_call(
        functools.partial(_att_in_kernel, row),
        out_shape=[jax.ShapeDtypeStruct((nqd, t), bf16), jax.ShapeDtypeStruct((t, nkd), bf16),
                   jax.ShapeDtypeStruct((nkd, t), bf16)],
        grid=(t // tm,),
        in_specs=[
            pl.BlockSpec((tm, D_MODEL), lambda i: (i, 0)),
            _resident((1, D_MODEL)),
            _mod_spec(layer, SHIFT_M),
            _mod_spec(layer, SCALE_M),
            _resident((nqd + 2 * nkd, D_MODEL)),
            gain_col, gain_col, gain_col, gain_col,
            row_t, col_t, row_t, col_t,
        ],
        out_specs=[pl.BlockSpec((nqd, tm), lambda i: (0, i)), pl.BlockSpec((tm, nkd), lambda i: (i, 0)),
                   pl.BlockSpec((nkd, tm), lambda i: (0, i))],
        compiler_params=_params("arbitrary"),
        name="att_in",
    )(x, gain, mods, mods, w_t, *gains, *tabs_t)


def _att_kernel(row, n_blocks, band, cast, *refs):
    ins, (o_ref,), scratch, cast_refs = _split_refs(refs, 13 if band else 7, 1, cast)
    s_refs, e_refs, yt_ref = scratch[:ATT_KV_HEADS], scratch[ATT_KV_HEADS:2 * ATT_KV_HEADS], scratch[-1]
    if band:
        sink_ref, qt_ref, kc_ref, vtc_ref, kp_ref, kn_ref, kx_ref, vtp_ref, vtn_ref, vtx_ref, w_ref, x_ref, gate_ref = ins
    else:
        sink_ref, qt_ref, kc_ref, vtc_ref, w_ref, x_ref, gate_ref = ins
    _cast_weights(cast_refs)
    i = pl.program_id(0)
    c = ATT_BLOCK
    g = ATT_GROUP
    hd = ATT_HEAD_DIM
    n_ctx = kc_ref.shape[0]
    if band:
        key = lax.broadcasted_iota(jnp.int32, (c, g * c), 0)
        qry = lax.broadcasted_iota(jnp.int32, (c, g * c), 1) % c
        prev_ok = (key >= qry + (c - ATT_WINDOW)) & (i > 0)
        next_ok = (key <= qry + (ATT_WINDOW - c)) & (i < n_blocks - 1)
    vals_t = []
    for kh in range(ATT_KV_HEADS):
        sl = slice((kh // 2) * LANES, (kh // 2 + 1) * LANES)
        vs = slice(kh * hd, (kh + 1) * hd)
        q_t = jnp.concatenate([qt_ref[(kh * g + gg) * hd:(kh * g + gg + 1) * hd, :] for gg in range(g)], axis=1)
        pad = jnp.zeros_like(q_t)
        q_t = jnp.concatenate([q_t, pad] if kh % 2 == 0 else [pad, q_t], axis=0)
        if band:
            keys = jnp.concatenate([kc_ref[:, sl], kp_ref[:, sl], kx_ref[:, sl], kn_ref[:, sl]], axis=0)
            vals_t.append(jnp.concatenate([vtc_ref[vs, :], vtp_ref[vs, :], vtx_ref[vs, :], vtn_ref[vs, :]], axis=1))
        else:
            keys = kc_ref[:, sl]
            vals_t.append(vtc_ref[vs, :])
        s_refs[kh][...] = jnp.dot(keys, q_t, preferred_element_type=f32)
    for kh in range(ATT_KV_HEADS):
        s = s_refs[kh][...]
        if band:
            s = jnp.concatenate([
                s[:n_ctx],
                jnp.where(prev_ok, s[n_ctx:n_ctx + c], NEG_INF),
                s[n_ctx + c:n_ctx + 2 * c],
                jnp.where(next_ok, s[n_ctx + 2 * c:], NEG_INF)], axis=0)
        sink = jnp.concatenate([jnp.full((1, c), sink_ref[kh * g + gg] * LOG2_E, f32) for gg in range(g)], axis=1)
        m = jnp.maximum(jnp.max(s, axis=0, keepdims=True), sink)
        e = jnp.exp2(s - m)
        denom = jnp.sum(e, axis=0, keepdims=True) + jnp.exp2(sink - m)
        e_refs[kh][...] = e.astype(bf16)
        o_t = jnp.dot(vals_t[kh], e_refs[kh][...], preferred_element_type=f32) * (1.0 / denom)
        for gg in range(g):
            h = kh * g + gg
            yt_ref[h * hd:(h + 1) * hd, :] = o_t[:, gg * c:(gg + 1) * c].astype(bf16)
    out = lax.dot_general(yt_ref[...], w_ref[...], (((0,), (0,)), ((), ())), preferred_element_type=f32)
    o_ref[...] = x_ref[...] + gate_ref[row:row + 1, :] * out


def _attention(sink, q_t, kc, vtc, kx, vtx, w_o, x, mods, layer, row, ffn_w=None):
    t = x.shape[0]
    cast = ffn_w is not None
    c_in, c_out, c_shape = _cast_io(layer) if cast else ([], [], [])
    c = ATT_BLOCK
    n = t // c
    nqd = ATT_Q_HEADS * ATT_HEAD_DIM
    nk = nv = ATT_KV_HEADS * ATT_HEAD_DIM
    band = kx is not None
    n_keys = kc.shape[0] + (3 * c if band else 0)
    tok = lambda w: pl.BlockSpec((c, w), lambda i: (i, 0))
    before = lambda i: jnp.maximum(i - 1, 0)
    after = lambda i: jnp.minimum(i + 1, n - 1)
    in_specs = [pl.BlockSpec(memory_space=pltpu.SMEM), pl.BlockSpec((nqd, c), lambda i: (0, i)),
                _resident(kc.shape), _resident(vtc.shape)]
    args = [sink, q_t, kc, vtc]
    if band:
        in_specs += [pl.BlockSpec((c, nk), lambda i: (before(i), 0)), pl.BlockSpec((c, nk), lambda i: (after(i), 0)),
                     tok(nk),
                     pl.BlockSpec((nv, c), lambda i: (0, before(i))), pl.BlockSpec((nv, c), lambda i: (0, after(i))),
                     pl.BlockSpec((nv, c), lambda i: (0, i))]
        args += [kx, kx, kx, vtx, vtx, vtx]
    in_specs += [_resident((nqd, D_MODEL)), tok(D_MODEL), _mod_spec(layer, GATE_M)] + c_in
    args += [w_o, x, mods, *(ffn_w or ())]
    return pl.pallas_call(
        functools.partial(_att_kernel, row, n, band, cast),
        out_shape=[jax.ShapeDtypeStruct((t, D_MODEL), f32)] + c_shape,
        grid=(n,),
        in_specs=in_specs,
        out_specs=[tok(D_MODEL)] + c_out,
        scratch_shapes=([pltpu.VMEM((n_keys, ATT_GROUP * c), f32)] * ATT_KV_HEADS
                        + [pltpu.VMEM((n_keys, ATT_GROUP * c), bf16)] * ATT_KV_HEADS
                        + [pltpu.VMEM((nqd, c), bf16)]),
        compiler_params=_params("arbitrary"),
        name="att_band" if band else "att_ctx",
    )(*args)


def _ffn_kernel(row, x_ref, g_ref, sh_ref, sc_ref, gate_ref, wgu_ref, wd_ref, o_ref):
    x = x_ref[...]
    h = _modulated(x, g_ref[...], sc_ref[row:row + 1, :], sh_ref[row:row + 1, :]).astype(bf16)
    acc = jnp.zeros(x.shape, f32)
    for c0 in range(0, FFN_HIDDEN, FFN_CHUNK):
        a = jnp.dot(h, wgu_ref[:, c0:c0 + FFN_CHUNK], preferred_element_type=f32)
        b = jnp.dot(h, wgu_ref[:, FFN_HIDDEN + c0:FFN_HIDDEN + c0 + FFN_CHUNK], preferred_element_type=f32)
        act = (_silu(a) * b).astype(bf16)
        acc = acc + jnp.dot(act, wd_ref[c0:c0 + FFN_CHUNK, :], preferred_element_type=f32)
    o_ref[...] = x + gate_ref[row:row + 1, :] * acc


def _ffn(x, mods, layer, row, gain, w_gu, w_down, tm):
    t = x.shape[0]
    tok = pl.BlockSpec((tm, D_MODEL), lambda i: (i, 0))
    return pl.pallas_call(
        functools.partial(_ffn_kernel, row),
        out_shape=jax.ShapeDtypeStruct((t, D_MODEL), f32),
        grid=(t // tm,),
        in_specs=[tok, _resident((1, D_MODEL)),
                  _mod_spec(layer, SHIFT_F), _mod_spec(layer, SCALE_F), _mod_spec(layer, GATE_F),
                  _resident((D_MODEL, 2 * FFN_HIDDEN)), _resident((FFN_HIDDEN, D_MODEL))],
        out_specs=tok,
        compiler_params=_params("arbitrary"),
        name="ffn",
    )(x, gain, mods, mods, mods, w_gu, w_down)


def _rope_tables(t, quarter, layout):
    n_rows = t // GRID_W
    inv = ROPE_BASE ** (-jnp.arange(quarter, dtype=f32) / quarter)
    ang = {'r': jnp.arange(n_rows).astype(f32)[:, None] * inv, 'c': jnp.arange(GRID_W).astype(f32)[:, None] * inv}
    n = {'r': n_rows, 'c': GRID_W}

    def table(axis, fn, signed):
        parts = []
        for grp in layout:
            if isinstance(grp, int):
                parts.append(jnp.zeros((n[axis], grp), f32))
            elif grp[0] == axis:
                sign = -1.0 if (signed and grp[1] == '1') else 1.0
                parts.append(sign * fn(ang[axis]))
            else:
                parts.append(jnp.zeros((n[axis], quarter), f32))
        return jnp.concatenate(parts, axis=1)

    return (table('r', jnp.cos, False), table('c', jnp.cos, False),
            table('r', jnp.sin, True), table('c', jnp.sin, True))


RET_ROPE_LAYOUT = ('r1', 'r2', 'c1', 'c2')
ATT_ROPE_LAYOUT = ('r1', 'r2', 'c1', 'c2')


def _transposed_tables(tabs, tm):
    gr = tm // GRID_W
    row_t = lambda a: a.T.reshape(a.shape[1], a.shape[0] // gr, gr).transpose(1, 0, 2)
    return row_t(tabs[0]), tabs[1].T, row_t(tabs[2]), tabs[3].T


def _identity_tables(t, width):
    n_rows = t // GRID_W
    return (jnp.ones((n_rows, width), f32), jnp.zeros((GRID_W, width), f32),
            jnp.zeros((n_rows, width), f32), jnp.zeros((GRID_W, width), f32))


def kernel(x, c, ctx, c_ctx, ada_w, ada_b, norm_mix, norm_ffn, conv_w1, conv_b1, conv_dw, conv_dw_b, conv_norm, conv_w2, ret_w_in, ret_decay_f, ret_decay_b, ret_w_out, att_w_qkv, att_q_norm, att_k_norm, att_sink, att_w_o, ffn_w_gu, ffn_w_down):
    assert x.shape[0] == 1 and c.shape[0] == 1 and ctx.shape[0] == 1
    t_lat, t_ctx = x.shape[1], ctx.shape[1]
    tm_lat, tm_ctx = 512, t_ctx
    xs, hc = x[0], ctx[0]
    lat, cx = 0, 1

    cond_t = jnp.zeros((D_MODEL, 8), f32).at[:, lat].set(c[0]).at[:, cx].set(c_ctx)
    mods = _adaln(cond_t, ada_w, ada_b)

    row1 = lambda v: v.reshape(1, -1)
    ffn_w = (ffn_w_gu, ffn_w_down)
    for i in range(DEPTH):
        kind, j, last = i % N_MIXERS, i // N_MIXERS, i == DEPTH - 1
        with_ctx = not last
        g_mix = row1(norm_mix[i])
        if kind == 0:
            w1, w2 = conv_w1[j].astype(bf16), conv_w2[j].astype(bf16)
            cargs = (g_mix, w1, row1(conv_b1[j]), conv_dw[j], row1(conv_dw_b[j]), row1(conv_norm[j]), w2)
            xs, w_gu, w_down = _conv_mixer(xs, mods, i, lat, *cargs, tm_lat, ffn_w=ffn_w)
            if with_ctx:
                hc, = _conv_mixer(hc, mods, i, cx, *cargs, tm_ctx)
        elif kind == 1:
            w_in, w_out = ret_w_in[j].astype(bf16), ret_w_out[j].astype(bf16)
            bdec = lambda d: jnp.broadcast_to(d[:, None, None], (RET_HEADS, 8, RET_V_DIM)).astype(f32)
            dec_f, dec_b = bdec(ret_decay_f[j]), bdec(ret_decay_b[j])
            zeros = jnp.zeros((RET_HEADS, RET_QK_DIM, RET_V_DIM), f32)
            qc, kc, vc, gc, sb_c, s0b = _ret_in(hc, mods, i, cx, g_mix, w_in, _identity_tables(t_ctx, RET_QK_DIM),
                                                dec_b, zeros, tm_ctx)
            hc_new, s0f = _ret_out(qc, kc, vc, gc, sb_c, zeros, dec_f, dec_b, w_out, hc, mods, i, cx)
            qx, kx, vx, gx, sb_x, _ = _ret_in(xs, mods, i, lat, g_mix, w_in,
                                              _rope_tables(t_lat, RET_QK_DIM // 4, RET_ROPE_LAYOUT), dec_b, s0b, tm_lat)
            xs, _, w_gu, w_down = _ret_out(qx, kx, vx, gx, sb_x, s0f, dec_f, dec_b, w_out, xs, mods, i, lat, ffn_w=ffn_w)
            if with_ctx:
                hc = hc_new
        else:
            w_t = att_w_qkv[j].T.astype(bf16)
            w_o = att_w_o[j].astype(bf16)
            q4 = ATT_HEAD_DIM // 4
            partner = lambda v: jnp.concatenate([v[q4:2 * q4], v[:q4], v[3 * q4:], v[2 * q4:3 * q4]])
            col = lambda v: v.astype(f32).reshape(-1, 1)
            gains = (col(att_q_norm[j]), col(partner(att_q_norm[j])), col(att_k_norm[j]), col(partner(att_k_norm[j])))
            sink = att_sink[j].astype(f32)
            tabs_c = _transposed_tables(_identity_tables(t_ctx, ATT_HEAD_DIM), tm_ctx)
            tabs_x = _transposed_tables(_rope_tables(t_lat, q4, ATT_ROPE_LAYOUT), tm_lat)
            qc, kc, vtc = _att_in(hc, mods, i, cx, g_mix, w_t, gains, tabs_c, tm_ctx)
            qx, kx, vtx = _att_in(xs, mods, i, lat, g_mix, w_t, gains, tabs_x, tm_lat)
            xs, w_gu, w_down = _attention(sink, qx, kc, vtc, kx, vtx, w_o, xs, mods, i, lat, ffn_w=ffn_w)
            if with_ctx:
                hc, = _attention(sink, qc, kc, vtc, None, None, w_o, hc, mods, i, cx)
        g_ffn = row1(norm_ffn[i])
        xs = _ffn(xs, mods, i, lat, g_ffn, w_gu, w_down, tm_lat)
        if with_ctx:
            hc = _ffn(hc, mods, i, cx, g_ffn, w_gu, w_down, tm_ctx)
    return xs[None]
```

```python
import functools

import jax
import jax.numpy as jnp
from jax import lax
from jax.experimental import pallas as pl
from jax.experimental.pallas import tpu as pltpu

f32 = jnp.float32
bf16 = jnp.bfloat16

D_MODEL = 1024
DEPTH = 4
N_COND = 2
GRID_W = 64
N_MIXERS = 3
CONV_WIDTH = 31
CONV_HALO = 16
CONV_ROWS = 128
CONV_SLOT = -(-(CONV_ROWS + 8) // 16) * 16
RET_HEADS = 4
RET_QK_DIM = 256
RET_V_DIM = 512
RET_CHUNK = 256
ATT_Q_HEADS = 16
ATT_KV_HEADS = 4
ATT_GROUP = ATT_Q_HEADS // ATT_KV_HEADS
ATT_HEAD_DIM = 64
ATT_WINDOW = 128
ATT_BLOCK = 128
FFN_HIDDEN = 2816
FFN_CHUNK = 256
CAST_GU_STEPS = 32
CAST_DN_STEPS = 16
ROPE_BASE = 10000.0
NORM_EPS = 1e-6
NEG_INF = -1e30
LOG2_E = 1.4426950408889634
LANES = 128
VMEM_LIMIT = 56 * 1024 * 1024

SHIFT_M, SCALE_M, GATE_M, SHIFT_F, SCALE_F, GATE_F = range(6)


def _params(*sem):
    return pltpu.CompilerParams(dimension_semantics=sem, vmem_limit_bytes=VMEM_LIMIT)


def _resident(shape):
    nd = len(shape)
    return pl.BlockSpec(shape, lambda *_: (0,) * nd, pipeline_mode=pl.Buffered(1))


def _mod_spec(layer, which):
    return pl.BlockSpec((None, 8, D_MODEL), lambda *_: (layer, 0, which))


def _modulated(x, gain, scale, shift):
    ms = jnp.mean(x * x, axis=-1, keepdims=True)
    return (x * lax.rsqrt(ms + NORM_EPS)) * (gain * (1.0 + scale)) + shift


def _silu(v):
    return v * jax.nn.sigmoid(v)


def _tile_table(row_ref, col_ref):
    return jnp.concatenate([row_ref[r:r + 1, :] + col_ref[...] for r in range(row_ref.shape[0])], axis=0)


def _table_specs(tm, width):
    row = pl.BlockSpec((tm // GRID_W, width), lambda i: (i, 0))
    col = _resident((GRID_W, width))
    return [row, col, row, col]


def _cast_io(layer):
    gu_rows = D_MODEL // CAST_GU_STEPS
    dn_rows = FFN_HIDDEN // CAST_DN_STEPS
    gu_i = lambda i: jnp.minimum(i, CAST_GU_STEPS - 1)
    dn_i = lambda i: jnp.minimum(i, CAST_DN_STEPS - 1)
    in_specs = [pl.BlockSpec((None, gu_rows, 2 * FFN_HIDDEN), lambda i: (layer, gu_i(i), 0)),
                pl.BlockSpec((None, dn_rows, D_MODEL), lambda i: (layer, dn_i(i), 0))]
    out_specs = [pl.BlockSpec((gu_rows, 2 * FFN_HIDDEN), lambda i: (gu_i(i), 0)),
                 pl.BlockSpec((dn_rows, D_MODEL), lambda i: (dn_i(i), 0))]
    out_shape = [jax.ShapeDtypeStruct((D_MODEL, 2 * FFN_HIDDEN), bf16),
                 jax.ShapeDtypeStruct((FFN_HIDDEN, D_MODEL), bf16)]
    return in_specs, out_specs, out_shape


def _cast_weights(cast_refs):
    if not cast_refs:
        return
    gu_ref, dn_ref, gu_out_ref, dn_out_ref = cast_refs
    i = pl.program_id(0)

    @pl.when(i < CAST_GU_STEPS)
    def _():
        gu_out_ref[...] = gu_ref[...].astype(bf16)

    @pl.when(i < CAST_DN_STEPS)
    def _():
        dn_out_ref[...] = dn_ref[...].astype(bf16)


def _split_refs(refs, n_in, n_out, cast):
    extra = 2 if cast else 0
    ins = refs[:n_in]
    outs = refs[n_in + extra:n_in + extra + n_out]
    scratch = refs[n_in + 2 * extra + n_out:]
    cast_refs = refs[n_in:n_in + extra] + refs[n_in + extra + n_out:n_in + 2 * extra + n_out]
    return ins, outs, scratch, cast_refs


def _adaln_kernel(ct_ref, w_ref, b_ref, o_ref):
    s = _silu(ct_ref[...])
    w = w_ref[...]
    rows = [jnp.sum(w * s[:, r:r + 1], axis=0, keepdims=True) for r in range(N_COND)]
    rows.append(jnp.zeros((8 - N_COND, w.shape[1]), f32))
    o_ref[...] = jnp.concatenate(rows, axis=0) + b_ref[...]


def _adaln(cond_t, ada_w, ada_b):
    tn = 1536
    n = 6 * D_MODEL
    return pl.pallas_call(
        _adaln_kernel,
        out_shape=jax.ShapeDtypeStruct((DEPTH, 8, n), f32),
        grid=(DEPTH, n // tn),
        in_specs=[
            pl.BlockSpec((D_MODEL, 8), lambda l, j: (0, 0)),
            pl.BlockSpec((None, D_MODEL, tn), lambda l, j: (l, 0, j)),
            pl.BlockSpec((None, 1, tn), lambda l, j: (l, 0, j)),
        ],
        out_specs=pl.BlockSpec((None, 8, tn), lambda l, j: (l, 0, j)),
        compiler_params=_params("arbitrary", "arbitrary"),
        name="adaln",
    )(cond_t, ada_w, ada_b.reshape(DEPTH, 1, n))


def _conv_kernel(row, n_tiles, cast, *refs):
    ins, (o_ref,), (xw_ref, win_ref, zs_ref, acc_ref, y_ref), cast_refs = _split_refs(refs, 14, 1, cast)
    (xm_ref, xp_ref, xn_ref, g_ref, sh_ref, sc_ref, w1_ref, b1_ref, dw_ref, dwb_ref, ng_ref, w2_ref, gate_ref,
     shift_ref) = ins
    _cast_weights(cast_refs)
    i = pl.program_id(0)
    tm = xm_ref.shape[0]
    rb = CONV_ROWS
    xw_ref[0:CONV_HALO, :] = xp_ref[...]
    xw_ref[CONV_HALO:CONV_HALO + tm, :] = xm_ref[...]
    xw_ref[CONV_HALO + tm:, :] = xn_ref[...]
    h = _modulated(xw_ref[...], g_ref[...], sc_ref[row:row + 1, :], sh_ref[row:row + 1, :])
    y = jnp.dot(h.astype(bf16), w1_ref[...], preferred_element_type=f32) + b1_ref[...]
    win_ref[...] = y[:, :D_MODEL] * jax.nn.sigmoid(y[:, D_MODEL:])

    @pl.when(i == 0)
    def _():
        win_ref[0:CONV_HALO, :] = jnp.zeros((CONV_HALO, D_MODEL), f32)

    @pl.when(i == n_tiles - 1)
    def _():
        win_ref[CONV_HALO + tm:, :] = jnp.zeros((CONV_HALO, D_MODEL), f32)

    off = CONV_HALO - CONV_WIDTH // 2

    @pl.when(i == 0)
    def _():
        zs_ref[...] = jnp.zeros(zs_ref.shape, bf16)

    def body(r, carry):
        base = pl.multiple_of(r * rb, rb)
        for lb in range(D_MODEL // LANES):
            ls = slice(lb * LANES, (lb + 1) * LANES)
            win = win_ref[pl.ds(base, rb + 2 * CONV_HALO), ls]
            for s in range(8):
                z = None
                for k in range(CONV_WIDTH):
                    if (off + k) % 8 != s:
                        continue
                    j = (off + k) // 8
                    term = win[8 * j:8 * j + rb + 8, :] * dw_ref[k:k + 1, ls]
                    z = term if z is None else z + term
                zs_ref[s * CONV_SLOT:s * CONV_SLOT + rb + 8, ls] = z.astype(bf16)
            if lb % 2 == 1:
                pr = slice((lb - 1) * LANES, (lb + 1) * LANES)
                acc_ref[:, pr] = jnp.dot(shift_ref[...], zs_ref[:, pr], preferred_element_type=f32) + dwb_ref[:, pr]
        acc = acc_ref[...]
        mu = jnp.mean(acc, axis=-1, keepdims=True)
        xc = acc - mu
        var = jnp.mean(xc * xc, axis=-1, keepdims=True)
        yn = xc * lax.rsqrt(var + NORM_EPS) * ng_ref[...]
        y_ref[pl.ds(base, rb), :] = _silu(yn).astype(bf16)
        return carry

    lax.fori_loop(0, tm // rb, body, 0)
    o = jnp.dot(y_ref[...], w2_ref[...], preferred_element_type=f32)
    o_ref[...] = xm_ref[...] + gate_ref[row:row + 1, :] * o


def _conv_shift_matrix():
    t = jnp.arange(CONV_ROWS)[:, None]
    col = jnp.arange(8 * CONV_SLOT)[None, :]
    return (col % CONV_SLOT == t + col // CONV_SLOT).astype(bf16)


def _conv_mixer(x, mods, layer, row, gain, w1, b1, dw, dw_b, norm_g, w2, tm, ffn_w=None):
    t = x.shape[0]
    cast = ffn_w is not None
    c_in, c_out, c_shape = _cast_io(layer) if cast else ([], [], [])
    n_tiles = t // tm
    hb = tm // CONV_HALO
    n_hb = t // CONV_HALO
    return pl.pallas_call(
        functools.partial(_conv_kernel, row, n_tiles, cast),
        out_shape=[jax.ShapeDtypeStruct((t, D_MODEL), f32)] + c_shape,
        grid=(n_tiles,),
        in_specs=[
            pl.BlockSpec((tm, D_MODEL), lambda i: (i, 0)),
            pl.BlockSpec((CONV_HALO, D_MODEL), lambda i: (jnp.maximum(i * hb - 1, 0), 0)),
            pl.BlockSpec((CONV_HALO, D_MODEL), lambda i: (jnp.minimum((i + 1) * hb, n_hb - 1), 0)),
            _resident((1, D_MODEL)),
            _mod_spec(layer, SHIFT_M),
            _mod_spec(layer, SCALE_M),
            _resident((D_MODEL, 2 * D_MODEL)),
            _resident((1, 2 * D_MODEL)),
            _resident((CONV_WIDTH, D_MODEL)),
            _resident((1, D_MODEL)),
            _resident((1, D_MODEL)),
            _resident((D_MODEL, D_MODEL)),
            _mod_spec(layer, GATE_M),
            _resident((CONV_ROWS, 8 * CONV_SLOT)),
        ] + c_in,
        out_specs=[pl.BlockSpec((tm, D_MODEL), lambda i: (i, 0))] + c_out,
        scratch_shapes=[
            pltpu.VMEM((tm + 2 * CONV_HALO, D_MODEL), f32),
            pltpu.VMEM((tm + 2 * CONV_HALO, D_MODEL), f32),
            pltpu.VMEM((8 * CONV_SLOT, D_MODEL), bf16),
            pltpu.VMEM((CONV_ROWS, D_MODEL), f32),
            pltpu.VMEM((tm, D_MODEL), bf16),
        ],
        compiler_params=_params("arbitrary"),
        name="conv_mixer",
    )(x, x, x, gain, mods, mods, w1, b1, dw, dw_b, norm_g, w2, mods, _conv_shift_matrix(), *(ffn_w or ()))


def _log_sigmoid(v):
    return jnp.minimum(v, 0.0) - jnp.log1p(jnp.exp(-jnp.abs(v)))


def _ret_in_kernel(row, x_ref, g_ref, sh_ref, sc_ref, w_ref, rcos_ref, ccos_ref, rsin_ref, csin_ref,
                   decb_ref, s0b_ref, q_ref, k_ref, v_ref, sg_ref, sball_ref, sb_ref):
    h = _modulated(x_ref[...], g_ref[...], sc_ref[row:row + 1, :], sh_ref[row:row + 1, :]).astype(bf16)
    hk = RET_HEADS * RET_QK_DIM
    hv = RET_HEADS * RET_V_DIM
    cos_t = _tile_table(rcos_ref, ccos_ref)
    sin_t = _tile_table(rsin_ref, csin_ref)

    def rope(y, scale):
        outs = []
        for b in range(hk // LANES):
            yb = y[:, b * LANES:(b + 1) * LANES]
            tb = (b % 2) * LANES
            rot = yb * cos_t[:, tb:tb + LANES] + pltpu.roll(yb, LANES // 2, axis=1) * sin_t[:, tb:tb + LANES]
            outs.append(rot * scale)
        return jnp.concatenate(outs, axis=1)

    q = jnp.dot(h, w_ref[:, 0:hk], preferred_element_type=f32)
    q_ref[...] = rope(q, 1.0).astype(bf16)
    k = rope(jnp.dot(h, w_ref[:, hk:2 * hk], preferred_element_type=f32), RET_QK_DIM ** -0.5).astype(bf16)
    k_ref[...] = k
    v = jnp.dot(h, w_ref[:, 2 * hk:2 * hk + hv], preferred_element_type=f32).astype(bf16)
    v_ref[...] = v
    g = jnp.dot(h, w_ref[:, 2 * hk + hv:], preferred_element_type=f32)
    sg_ref[...] = _silu(g).astype(bf16)

    c = RET_CHUNK

    @pl.when(pl.program_id(0) == 0)
    def _():
        sb_ref[...] = s0b_ref[...]

    idx = lax.broadcasted_iota(jnp.int32, (c, 1), 0).astype(f32)
    for ci in reversed(range(x_ref.shape[0] // c)):
        rows = slice(ci * c, (ci + 1) * c)
        for hh in range(RET_HEADS):
            lg_b = _log_sigmoid(decb_ref[hh, 0:1, :])
            kd = jnp.exp(lg_b[:, 0:1] * idx)
            kh = (k[rows, hh * RET_QK_DIM:(hh + 1) * RET_QK_DIM].astype(f32) * kd).astype(bf16)
            a = lax.dot_general(kh, v[rows, hh * RET_V_DIM:(hh + 1) * RET_V_DIM], (((0,), (0,)), ((), ())),
                                preferred_element_type=f32)
            s = sb_ref[hh]
            sball_ref[ci, hh] = s.astype(bf16)
            sb_ref[hh] = s * jnp.exp(lg_b * c) + a


def _ret_in(x, mods, layer, row, gain, w_in, tabs, dec_b, s0b, tm):
    t = x.shape[0]
    n = t // tm
    hk = RET_HEADS * RET_QK_DIM
    hv = RET_HEADS * RET_V_DIM
    st = (RET_HEADS, RET_QK_DIM, RET_V_DIM)
    tok = lambda w: pl.BlockSpec((tm, w), lambda i: (n - 1 - i, 0))
    row_tab = pl.BlockSpec((tm // GRID_W, RET_QK_DIM), lambda i: (n - 1 - i, 0))
    col_tab = _resident((GRID_W, RET_QK_DIM))
    return pl.pallas_call(
        functools.partial(_ret_in_kernel, row),
        out_shape=[jax.ShapeDtypeStruct((t, hk), bf16), jax.ShapeDtypeStruct((t, hk), bf16),
                   jax.ShapeDtypeStruct((t, hv), bf16), jax.ShapeDtypeStruct((t, hv), bf16),
                   jax.ShapeDtypeStruct((t // RET_CHUNK,) + st, bf16), jax.ShapeDtypeStruct(st, f32)],
        grid=(n,),
        in_specs=[
            tok(D_MODEL),
            _resident((1, D_MODEL)),
            _mod_spec(layer, SHIFT_M),
            _mod_spec(layer, SCALE_M),
            _resident((D_MODEL, 2 * hk + 2 * hv)),
            row_tab, col_tab, row_tab, col_tab,
            _resident((RET_HEADS, 8, RET_V_DIM)),
            _resident(st),
        ],
        out_specs=[tok(hk), tok(hk), tok(hv), tok(hv),
                   pl.BlockSpec((tm // RET_CHUNK,) + st, lambda i: (n - 1 - i, 0, 0, 0)),
                   pl.BlockSpec(st, lambda i: (0, 0, 0))],
        compiler_params=_params("arbitrary"),
        name="ret_in",
    )(x, gain, mods, mods, w_in, *tabs, dec_b, s0b)


def _ret_out_kernel(row, cast, *refs):
    ins, (o_ref, sf_ref), (dec_ref, y_ref), cast_refs = _split_refs(refs, 11, 2, cast)
    q_ref, k_ref, v_ref, sg_ref, sb_ref, s0f_ref, decf_ref, decb_ref, w_ref, x_ref, gate_ref = ins
    _cast_weights(cast_refs)
    c = RET_CHUNK

    @pl.when(pl.program_id(0) == 0)
    def _():
        sf_ref[...] = s0f_ref[...]
        t_i = lax.broadcasted_iota(jnp.int32, (c, c), 0)
        m_i = lax.broadcasted_iota(jnp.int32, (c, c), 1)
        rel = (t_i - m_i).astype(f32)
        for h in range(RET_HEADS):
            lg_f = _log_sigmoid(decf_ref[h, 0:1, :c])
            lg_b = _log_sigmoid(decb_ref[h, 0:1, :c])
            d_f = jnp.where(rel >= 0, jnp.exp(lg_f * jnp.maximum(rel, 0.0)), 0.0)
            d_b = jnp.where(rel <= 0, jnp.exp(lg_b * jnp.maximum(-rel, 0.0)), 0.0)
            dec_ref[h] = d_f + d_b

    idx = lax.broadcasted_iota(jnp.int32, (c, 1), 0).astype(f32)
    for h in range(RET_HEADS):
        lg_f = _log_sigmoid(decf_ref[h, 0:1, :])
        lg_b = _log_sigmoid(decb_ref[h, 0:1, :])
        q = q_ref[:, h * RET_QK_DIM:(h + 1) * RET_QK_DIM]
        k = k_ref[:, h * RET_QK_DIM:(h + 1) * RET_QK_DIM]
        v = v_ref[:, h * RET_V_DIM:(h + 1) * RET_V_DIM]
        s_f = sf_ref[h]
        s = lax.dot_general(q, k, (((1,), (1,)), ((), ())), preferred_element_type=f32)
        o = jnp.dot((s * dec_ref[h]).astype(bf16), v, preferred_element_type=f32)
        o = o + jnp.exp(lg_f[:, 0:1] * (idx + 1.0)) * jnp.dot(q, s_f.astype(bf16), preferred_element_type=f32)
        o = o + jnp.exp(lg_b[:, 0:1] * (c - idx)) * jnp.dot(q, sb_ref[h], preferred_element_type=f32)
        o = o * lax.rsqrt(jnp.mean(o * o, axis=-1, keepdims=True) + NORM_EPS)
        sg = sg_ref[:, h * RET_V_DIM:(h + 1) * RET_V_DIM].astype(f32)
        y_ref[:, h * RET_V_DIM:(h + 1) * RET_V_DIM] = (sg * o).astype(bf16)
        kd = (k.astype(f32) * jnp.exp(lg_f[:, 0:1] * (c - 1.0 - idx))).astype(bf16)
        a = lax.dot_general(kd, v, (((0,), (0,)), ((), ())), preferred_element_type=f32)
        sf_ref[h] = s_f * jnp.exp(lg_f * c) + a
    out = jnp.dot(y_ref[...], w_ref[...], preferred_element_type=f32)
    o_ref[...] = x_ref[...] + gate_ref[row:row + 1, :] * out


def _ret_out(q, k, v, sg, sb_all, s0f, dec_f, dec_b, w_out, x, mods, layer, row, ffn_w=None):
    t = x.shape[0]
    cast = ffn_w is not None
    c_in, c_out, c_shape = _cast_io(layer) if cast else ([], [], [])
    n = t // RET_CHUNK
    hk = RET_HEADS * RET_QK_DIM
    hv = RET_HEADS * RET_V_DIM
    st = (RET_HEADS, RET_QK_DIM, RET_V_DIM)
    tok = lambda w: pl.BlockSpec((RET_CHUNK, w), lambda i: (i, 0))
    return pl.pallas_call(
        functools.partial(_ret_out_kernel, row, cast),
        out_shape=[jax.ShapeDtypeStruct((t, D_MODEL), f32), jax.ShapeDtypeStruct(st, f32)] + c_shape,
        grid=(n,),
        in_specs=[tok(hk), tok(hk), tok(hv), tok(hv),
                  pl.BlockSpec((None,) + st, lambda i: (i, 0, 0, 0)),
                  _resident(st),
                  _resident((RET_HEADS, 8, RET_V_DIM)), _resident((RET_HEADS, 8, RET_V_DIM)),
                  _resident((hv, D_MODEL)),
                  tok(D_MODEL),
                  _mod_spec(layer, GATE_M)] + c_in,
        out_specs=[tok(D_MODEL), pl.BlockSpec(st, lambda i: (0, 0, 0))] + c_out,
        scratch_shapes=[pltpu.VMEM((RET_HEADS, RET_CHUNK, RET_CHUNK), f32),
                        pltpu.VMEM((RET_CHUNK, hv), bf16)],
        compiler_params=_params("arbitrary"),
        name="ret_out",
    )(q, k, v, sg, sb_all, s0f, dec_f, dec_b, w_out, x, mods, *(ffn_w or ()))


def _tile_table_t(row_ref, col_ref):
    rt, ct = row_ref[...], col_ref[...]
    return jnp.concatenate([rt[:, r:r + 1] + ct for r in range(rt.shape[1])], axis=1)


def _att_in_kernel(row, x_ref, g_ref, sh_ref, sc_ref, w_ref, qg_ref, qgp_ref, kg_ref, kgp_ref,
                   rcos_ref, ccos_ref, rsin_ref, csin_ref, qt_ref, k_ref, vt_ref):
    h = _modulated(x_ref[...], g_ref[...], sc_ref[row:row + 1, :], sh_ref[row:row + 1, :]).astype(bf16)
    hd = ATT_HEAD_DIM
    q4 = hd // 4
    nqd = ATT_Q_HEADS * hd
    nkd = ATT_KV_HEADS * hd

    yt = lax.dot_general(w_ref[...], h, (((1,), (1,)), ((), ())), preferred_element_type=f32)
    vt_ref[...] = yt[nqd + nkd:, :].astype(bf16)

    def partner_rows(t):
        return jnp.concatenate([t[q4:2 * q4], t[0:q4], t[3 * q4:], t[2 * q4:3 * q4]], axis=0)

    cos_t = _tile_table_t(rcos_ref, ccos_ref)
    sin_t = _tile_table_t(rsin_ref, csin_ref)

    def norm_rope(y, cos_g, sin_g):
        ms = jnp.sum(y * y, axis=0, keepdims=True) * (1.0 / hd)
        return (y * cos_g + partner_rows(y) * sin_g) * lax.rsqrt(ms + NORM_EPS)

    scale = hd ** -0.5 * LOG2_E
    cos_q, sin_q = cos_t * (qg_ref[...] * scale), sin_t * (qgp_ref[...] * scale)
    for hh in range(ATT_Q_HEADS):
        qt_ref[hh * hd:(hh + 1) * hd, :] = norm_rope(yt[hh * hd:(hh + 1) * hd, :], cos_q, sin_q).astype(bf16)
    cos_k, sin_k = cos_t * kg_ref[...], sin_t * kgp_ref[...]
    k_t = jnp.concatenate([norm_rope(yt[nqd + kh * hd:nqd + (kh + 1) * hd, :], cos_k, sin_k)
                           for kh in range(ATT_KV_HEADS)], axis=0)
    k_ref[...] = k_t.T.astype(bf16)


def _att_in(x, mods, layer, row, gain, w_t, gains, tabs_t, tm):
    t = x.shape[0]
    nqd = ATT_Q_HEADS * ATT_HEAD_DIM
    nkd = ATT_KV_HEADS * ATT_HEAD_DIM
    gr = tm // GRID_W
    row_t = pl.BlockSpec((None, ATT_HEAD_DIM, gr), lambda i: (i, 0, 0))
    col_t = _resident((ATT_HEAD_DIM, GRID_W))
    gain_col = _resident((ATT_HEAD_DIM, 1))
    return pl.pallas_call(
        functools.partial(_att_in_kernel, row),
        out_shape=[jax.ShapeDtypeStruct((nqd, t), bf16), jax.ShapeDtypeStruct((t, nkd), bf16),
                   jax.ShapeDtypeStruct((nkd, t), bf16)],
        grid=(t // tm,),
        in_specs=[
            pl.BlockSpec((tm, D_MODEL), lambda i: (i, 0)),
            _resident((1, D_MODEL)),
            _mod_spec(layer, SHIFT_M),
            _mod_spec(layer, SCALE_M),
            _resident((nqd + 2 * nkd, D_MODEL)),
            gain_col, gain_col, gain_col, gain_col,
            row_t, col_t, row_t, col_t,
        ],
        out_specs=[pl.BlockSpec((nqd, tm), lambda i: (0, i)), pl.BlockSpec((tm, nkd), lambda i: (i, 0)),
                   pl.BlockSpec((nkd, tm), lambda i: (0, i))],
        compiler_params=_params("arbitrary"),
        name="att_in",
    )(x, gain, mods, mods, w_t, *gains, *tabs_t)


def _att_kernel(row, n_blocks, band, cast, *refs):
    ins, (o_ref,), scratch, cast_refs = _split_refs(refs, 13 if band else 7, 1, cast)
    s_refs, e_refs, yt_ref = scratch[:ATT_KV_HEADS], scratch[ATT_KV_HEADS:2 * ATT_KV_HEADS], scratch[-1]
    if band:
        sink_ref, qt_ref, kc_ref, vtc_ref, kp_ref, kn_ref, kx_ref, vtp_ref, vtn_ref, vtx_ref, w_ref, x_ref, gate_ref = ins
    else:
        sink_ref, qt_ref, kc_ref, vtc_ref, w_ref, x_ref, gate_ref = ins
    _cast_weights(cast_refs)
    i = pl.program_id(0)
    c = ATT_BLOCK
    g = ATT_GROUP
    hd = ATT_HEAD_DIM
    n_ctx = kc_ref.shape[0]
    if band:
        key = lax.broadcasted_iota(jnp.int32, (c, g * c), 0)
        qry = lax.broadcasted_iota(jnp.int32, (c, g * c), 1) % c
        prev_ok = (key >= qry + (c - ATT_WINDOW)) & (i > 0)
        next_ok = (key <= qry + (ATT_WINDOW - c)) & (i < n_blocks - 1)
    vals_t = []
    for kh in range(ATT_KV_HEADS):
        sl = slice((kh // 2) * LANES, (kh // 2 + 1) * LANES)
        vs = slice(kh * hd, (kh + 1) * hd)
        q_t = jnp.concatenate([qt_ref[(kh * g + gg) * hd:(kh * g + gg + 1) * hd, :] for gg in range(g)], axis=1)
        pad = jnp.zeros_like(q_t)
        q_t = jnp.concatenate([q_t, pad] if kh % 2 == 0 else [pad, q_t], axis=0)
        if band:
            keys = jnp.concatenate([kc_ref[:, sl], kp_ref[:, sl], kx_ref[:, sl], kn_ref[:, sl]], axis=0)
            vals_t.append(jnp.concatenate([vtc_ref[vs, :], vtp_ref[vs, :], vtx_ref[vs, :], vtn_ref[vs, :]], axis=1))
        else:
            keys = kc_ref[:, sl]
            vals_t.append(vtc_ref[vs, :])
        s_refs[kh][...] = jnp.dot(keys, q_t, preferred_element_type=f32)
    for kh in range(ATT_KV_HEADS):
        s = s_refs[kh][...]
        if band:
            s = jnp.concatenate([
                s[:n_ctx],
                jnp.where(prev_ok, s[n_ctx:n_ctx + c], NEG_INF),
                s[n_ctx + c:n_ctx + 2 * c],
                jnp.where(next_ok, s[n_ctx + 2 * c:], NEG_INF)], axis=0)
        sink = jnp.concatenate([jnp.full((1, c), sink_ref[kh * g + gg] * LOG2_E, f32) for gg in range(g)], axis=1)
        m = jnp.maximum(jnp.max(s, axis=0, keepdims=True), sink)
        e = jnp.exp2(s - m)
        denom = jnp.sum(e, axis=0, keepdims=True) + jnp.exp2(sink - m)
        e_refs[kh][...] = e.astype(bf16)
        o_t = jnp.dot(vals_t[kh], e_refs[kh][...], preferred_element_type=f32) * (1.0 / denom)
        for gg in range(g):
            h = kh * g + gg
            yt_ref[h * hd:(h + 1) * hd, :] = o_t[:, gg * c:(gg + 1) * c].astype(bf16)
    out = lax.dot_general(yt_ref[...], w_ref[...], (((0,), (0,)), ((), ())), preferred_element_type=f32)
    o_ref[...] = x_ref[...] + gate_ref[row:row + 1, :] * out


def _attention(sink, q_t, kc, vtc, kx, vtx, w_o, x, mods, layer, row, ffn_w=None):
    t = x.shape[0]
    cast = ffn_w is not None
    c_in, c_out, c_shape = _cast_io(layer) if cast else ([], [], [])
    c = ATT_BLOCK
    n = t // c
    nqd = ATT_Q_HEADS * ATT_HEAD_DIM
    nk = nv = ATT_KV_HEADS * ATT_HEAD_DIM
    band = kx is not None
    n_keys = kc.shape[0] + (3 * c if band else 0)
    tok = lambda w: pl.BlockSpec((c, w), lambda i: (i, 0))
    before = lambda i: jnp.maximum(i - 1, 0)
    after = lambda i: jnp.minimum(i + 1, n - 1)
    in_specs = [pl.BlockSpec(memory_space=pltpu.SMEM), pl.BlockSpec((nqd, c), lambda i: (0, i)),
                _resident(kc.shape), _resident(vtc.shape)]
    args = [sink, q_t, kc, vtc]
    if band:
        in_specs += [pl.BlockSpec((c, nk), lambda i: (before(i), 0)), pl.BlockSpec((c, nk), lambda i: (after(i), 0)),
                     tok(nk),
                     pl.BlockSpec((nv, c), lambda i: (0, before(i))), pl.BlockSpec((nv, c), lambda i: (0, after(i))),
                     pl.BlockSpec((nv, c), lambda i: (0, i))]
        args += [kx, kx, kx, vtx, vtx, vtx]
    in_specs += [_resident((nqd, D_MODEL)), tok(D_MODEL), _mod_spec(layer, GATE_M)] + c_in
    args += [w_o, x, mods, *(ffn_w or ())]
    return pl.pallas_call(
        functools.partial(_att_kernel, row, n, band, cast),
        out_shape=[jax.ShapeDtypeStruct((t, D_MODEL), f32)] + c_shape,
        grid=(n,),
        in_specs=in_specs,
        out_specs=[tok(D_MODEL)] + c_out,
        scratch_shapes=([pltpu.VMEM((n_keys, ATT_GROUP * c), f32)] * ATT_KV_HEADS
                        + [pltpu.VMEM((n_keys, ATT_GROUP * c), bf16)] * ATT_KV_HEADS
                        + [pltpu.VMEM((nqd, c), bf16)]),
        compiler_params=_params("arbitrary"),
        name="att_band" if band else "att_ctx",
    )(*args)


def _ffn_kernel(row, x_ref, g_ref, sh_ref, sc_ref, gate_ref, wgu_ref, wd_ref, o_ref):
    x = x_ref[...]
    h = _modulated(x, g_ref[...], sc_ref[row:row + 1, :], sh_ref[row:row + 1, :]).astype(bf16)
    acc = jnp.zeros(x.shape, f32)
    for c0 in range(0, FFN_HIDDEN, FFN_CHUNK):
        a = jnp.dot(h, wgu_ref[:, c0:c0 + FFN_CHUNK], preferred_element_type=f32)
        b = jnp.dot(h, wgu_ref[:, FFN_HIDDEN + c0:FFN_HIDDEN + c0 + FFN_CHUNK], preferred_element_type=f32)
        act = (_silu(a) * b).astype(bf16)
        acc = acc + jnp.dot(act, wd_ref[c0:c0 + FFN_CHUNK, :], preferred_element_type=f32)
    o_ref[...] = x + gate_ref[row:row + 1, :] * acc


def _ffn(x, mods, layer, row, gain, w_gu, w_down, tm):
    t = x.shape[0]
    tok = pl.BlockSpec((tm, D_MODEL), lambda i: (i, 0))
    return pl.pallas_call(
        functools.partial(_ffn_kernel, row),
        out_shape=jax.ShapeDtypeStruct((t, D_MODEL), f32),
        grid=(t // tm,),
        in_specs=[tok, _resident((1, D_MODEL)),
                  _mod_spec(layer, SHIFT_F), _mod_spec(layer, SCALE_F), _mod_spec(layer, GATE_F),
                  _resident((D_MODEL, 2 * FFN_HIDDEN)), _resident((FFN_HIDDEN, D_MODEL))],
        out_specs=tok,
        compiler_params=_params("arbitrary"),
        name="ffn",
    )(x, gain, mods, mods, mods, w_gu, w_down)


def _rope_tables(t, quarter, layout):
    n_rows = t // GRID_W
    inv = ROPE_BASE ** (-jnp.arange(quarter, dtype=f32) / quarter)
    ang = {'r': jnp.arange(n_rows).astype(f32)[:, None] * inv, 'c': jnp.arange(GRID_W).astype(f32)[:, None] * inv}
    n = {'r': n_rows, 'c': GRID_W}

    def table(axis, fn, signed):
        parts = []
        for grp in layout:
            if isinstance(grp, int):
                parts.append(jnp.zeros((n[axis], grp), f32))
            elif grp[0] == axis:
                sign = -1.0 if (signed and grp[1] == '1') else 1.0
                parts.append(sign * fn(ang[axis]))
            else:
                parts.append(jnp.zeros((n[axis], quarter), f32))
        return jnp.concatenate(parts, axis=1)

    return (table('r', jnp.cos, False), table('c', jnp.cos, False),
            table('r', jnp.sin, True), table('c', jnp.sin, True))


RET_ROPE_LAYOUT = ('r1', 'r2', 'c1', 'c2')
ATT_ROPE_LAYOUT = ('r1', 'r2', 'c1', 'c2')


def _transposed_tables(tabs, tm):
    gr = tm // GRID_W
    row_t = lambda a: a.T.reshape(a.shape[1], a.shape[0] // gr, gr).transpose(1, 0, 2)
    return row_t(tabs[0]), tabs[1].T, row_t(tabs[2]), tabs[3].T


def _identity_tables(t, width):
    n_rows = t // GRID_W
    return (jnp.ones((n_rows, width), f32), jnp.zeros((GRID_W, width), f32),
            jnp.zeros((n_rows, width), f32), jnp.zeros((GRID_W, width), f32))


def kernel(x, c, ctx, c_ctx, ada_w, ada_b, norm_mix, norm_ffn, conv_w1, conv_b1, conv_dw, conv_dw_b, conv_norm, conv_w2, ret_w_in, ret_decay_f, ret_decay_b, ret_w_out, att_w_qkv, att_q_norm, att_k_norm, att_sink, att_w_o, ffn_w_gu, ffn_w_down):
    assert x.shape[0] == 1 and c.shape[0] == 1 and ctx.shape[0] == 1
    t_lat, t_ctx = x.shape[1], ctx.shape[1]
    tm_lat, tm_ctx = 512, t_ctx
    xs, hc = x[0], ctx[0]
    lat, cx = 0, 1

    cond_t = jnp.zeros((D_MODEL, 8), f32).at[:, lat].set(c[0]).at[:, cx].set(c_ctx)
    mods = _adaln(cond_t, ada_w, ada_b)

    row1 = lambda v: v.reshape(1, -1)
    ffn_w = (ffn_w_gu, ffn_w_down)
    for i in range(DEPTH):
        kind, j, last = i % N_MIXERS, i // N_MIXERS, i == DEPTH - 1
        with_ctx = not last
        g_mix = row1(norm_mix[i])
        if kind == 0:
            w1, w2 = conv_w1[j].astype(bf16), conv_w2[j].astype(bf16)
            cargs = (g_mix, w1, row1(conv_b1[j]), conv_dw[j], row1(conv_dw_b[j]), row1(conv_norm[j]), w2)
            xs, w_gu, w_down = _conv_mixer(xs, mods, i, lat, *cargs, tm_lat, ffn_w=ffn_w)
            if with_ctx:
                hc, = _conv_mixer(hc, mods, i, cx, *cargs, tm_ctx)
        elif kind == 1:
            w_in, w_out = ret_w_in[j].astype(bf16), ret_w_out[j].astype(bf16)
            bdec = lambda d: jnp.broadcast_to(d[:, None, None], (RET_HEADS, 8, RET_V_DIM)).astype(f32)
            dec_f, dec_b = bdec(ret_decay_f[j]), bdec(ret_decay_b[j])
            zeros = jnp.zeros((RET_HEADS, RET_QK_DIM, RET_V_DIM), f32)
            qc, kc, vc, gc, sb_c, s0b = _ret_in(hc, mods, i, cx, g_mix, w_in, _identity_tables(t_ctx, RET_QK_DIM),
                                                dec_b, zeros, tm_ctx)
            hc_new, s0f = _ret_out(qc, kc, vc, gc, sb_c, zeros, dec_f, dec_b, w_out, hc, mods, i, cx)
            qx, kx, vx, gx, sb_x, _ = _ret_in(xs, mods, i, lat, g_mix, w_in,
                                              _rope_tables(t_lat, RET_QK_DIM // 4, RET_ROPE_LAYOUT), dec_b, s0b, tm_lat)
            xs, _, w_gu, w_down = _ret_out(qx, kx, vx, gx, sb_x, s0f, dec_f, dec_b, w_out, xs, mods, i, lat, ffn_w=ffn_w)
            if with_ctx:
                hc = hc_new
        else:
            w_t = att_w_qkv[j].T.astype(bf16)
            w_o = att_w_o[j].astype(bf16)
            q4 = ATT_HEAD_DIM // 4
            partner = lambda v: jnp.concatenate([v[q4:2 * q4], v[:q4], v[3 * q4:], v[2 * q4:3 * q4]])
            col = lambda v: v.astype(f32).reshape(-1, 1)
            gains = (col(att_q_norm[j]), col(partner(att_q_norm[j])), col(att_k_norm[j]), col(partner(att_k_norm[j])))
            sink = att_sink[j].astype(f32)
            tabs_c = _transposed_tables(_identity_tables(t_ctx, ATT_HEAD_DIM), tm_ctx)
            tabs_x = _transposed_tables(_rope_tables(t_lat, q4, ATT_ROPE_LAYOUT), tm_lat)
            qc, kc, vtc = _att_in(hc, mods, i, cx, g_mix, w_t, gains, tabs_c, tm_ctx)
            qx, kx, vtx = _att_in(xs, mods, i, lat, g_mix, w_t, gains, tabs_x, tm_lat)
            xs, w_gu, w_down = _attention(sink, qx, kc, vtc, kx, vtx, w_o, xs, mods, i, lat, ffn_w=ffn_w)
            if with_ctx:
                hc, = _attention(sink, qc, kc, vtc, None, None, w_o, hc, mods, i, cx)
        g_ffn = row1(norm_ffn[i])
        xs = _ffn(xs, mods, i, lat, g_ffn, w_gu, w_down, tm_lat)
        if with_ctx:
            hc = _ffn(hc, mods, i, cx, g_ffn, w_gu, w_down, tm_ctx)
    return xs[None]
```

```python
import functools

import jax
import jax.numpy as jnp
from jax import lax
from jax.experimental import pallas as pl
from jax.experimental.pallas import tpu as pltpu

f32 = jnp.float32
bf16 = jnp.bfloat16

D_MODEL = 1024
DEPTH = 4
N_COND = 2
GRID_W = 64
N_MIXERS = 3
CONV_WIDTH = 31
CONV_HALO = 16
CONV_ROWS = 128
RET_HEADS = 4
RET_QK_DIM = 256
RET_V_DIM = 512
RET_CHUNK = 256
ATT_Q_HEADS = 16
ATT_KV_HEADS = 4
ATT_GROUP = ATT_Q_HEADS // ATT_KV_HEADS
ATT_HEAD_DIM = 64
ATT_WINDOW = 128
ATT_BLOCK = 128
FFN_HIDDEN = 2816
FFN_CHUNK = 256
CAST_GU_STEPS = 32
CAST_DN_STEPS = 16
ROPE_BASE = 10000.0
NORM_EPS = 1e-6
NEG_INF = -1e30
LOG2_E = 1.4426950408889634
LANES = 128
VMEM_LIMIT = 56 * 1024 * 1024

SHIFT_M, SCALE_M, GATE_M, SHIFT_F, SCALE_F, GATE_F = range(6)


def _params(*sem):
    return pltpu.CompilerParams(dimension_semantics=sem, vmem_limit_bytes=VMEM_LIMIT)


def _resident(shape):
    nd = len(shape)
    return pl.BlockSpec(shape, lambda *_: (0,) * nd, pipeline_mode=pl.Buffered(1))


def _mod_spec(layer, which):
    return pl.BlockSpec((None, 8, D_MODEL), lambda *_: (layer, 0, which))


def _modulated(x, gain, scale, shift):
    ms = jnp.mean(x * x, axis=-1, keepdims=True)
    return (x * lax.rsqrt(ms + NORM_EPS)) * (gain * (1.0 + scale)) + shift


def _silu(v):
    return v * jax.nn.sigmoid(v)


def _tile_table(row_ref, col_ref):
    return jnp.concatenate([row_ref[r:r + 1, :] + col_ref[...] for r in range(row_ref.shape[0])], axis=0)


def _table_specs(tm, width):
    row = pl.BlockSpec((tm // GRID_W, width), lambda i: (i, 0))
    col = _resident((GRID_W, width))
    return [row, col, row, col]


def _cast_io(layer):
    gu_rows = D_MODEL // CAST_GU_STEPS
    dn_rows = FFN_HIDDEN // CAST_DN_STEPS
    gu_i = lambda i: jnp.minimum(i, CAST_GU_STEPS - 1)
    dn_i = lambda i: jnp.minimum(i, CAST_DN_STEPS - 1)
    in_specs = [pl.BlockSpec((None, gu_rows, 2 * FFN_HIDDEN), lambda i: (layer, gu_i(i), 0)),
                pl.BlockSpec((None, dn_rows, D_MODEL), lambda i: (layer, dn_i(i), 0))]
    out_specs = [pl.BlockSpec((gu_rows, 2 * FFN_HIDDEN), lambda i: (gu_i(i), 0)),
                 pl.BlockSpec((dn_rows, D_MODEL), lambda i: (dn_i(i), 0))]
    out_shape = [jax.ShapeDtypeStruct((D_MODEL, 2 * FFN_HIDDEN), bf16),
                 jax.ShapeDtypeStruct((FFN_HIDDEN, D_MODEL), bf16)]
    return in_specs, out_specs, out_shape


def _cast_weights(cast_refs):
    if not cast_refs:
        return
    gu_ref, dn_ref, gu_out_ref, dn_out_ref = cast_refs
    i = pl.program_id(0)

    @pl.when(i < CAST_GU_STEPS)
    def _():
        gu_out_ref[...] = gu_ref[...].astype(bf16)

    @pl.when(i < CAST_DN_STEPS)
    def _():
        dn_out_ref[...] = dn_ref[...].astype(bf16)


def _split_refs(refs, n_in, n_out, cast):
    extra = 2 if cast else 0
    ins = refs[:n_in]
    outs = refs[n_in + extra:n_in + extra + n_out]
    scratch = refs[n_in + 2 * extra + n_out:]
    cast_refs = refs[n_in:n_in + extra] + refs[n_in + extra + n_out:n_in + 2 * extra + n_out]
    return ins, outs, scratch, cast_refs


def _adaln_kernel(ct_ref, w_ref, b_ref, o_ref):
    s = _silu(ct_ref[...])
    w = w_ref[...]
    rows = [jnp.sum(w * s[:, r:r + 1], axis=0, keepdims=True) for r in range(N_COND)]
    rows.append(jnp.zeros((8 - N_COND, w.shape[1]), f32))
    o_ref[...] = jnp.concatenate(rows, axis=0) + b_ref[...]


def _adaln(cond_t, ada_w, ada_b):
    tn = 1536
    n = 6 * D_MODEL
    return pl.pallas_call(
        _adaln_kernel,
        out_shape=jax.ShapeDtypeStruct((DEPTH, 8, n), f32),
        grid=(DEPTH, n // tn),
        in_specs=[
            pl.BlockSpec((D_MODEL, 8), lambda l, j: (0, 0)),
            pl.BlockSpec((None, D_MODEL, tn), lambda l, j: (l, 0, j)),
            pl.BlockSpec((None, 1, tn), lambda l, j: (l, 0, j)),
        ],
        out_specs=pl.BlockSpec((None, 8, tn), lambda l, j: (l, 0, j)),
        compiler_params=_params("arbitrary", "arbitrary"),
        name="adaln",
    )(cond_t, ada_w, ada_b.reshape(DEPTH, 1, n))


def _conv_kernel(row, n_tiles, cast, *refs):
    ins, (o_ref,), (xw_ref, win_ref, acc_ref, y_ref), cast_refs = _split_refs(refs, 13, 1, cast)
    xm_ref, xp_ref, xn_ref, g_ref, sh_ref, sc_ref, w1_ref, b1_ref, dw_ref, dwb_ref, ng_ref, w2_ref, gate_ref = ins
    _cast_weights(cast_refs)
    i = pl.program_id(0)
    tm = xm_ref.shape[0]
    rb = CONV_ROWS
    xw_ref[0:CONV_HALO, :] = xp_ref[...]
    xw_ref[CONV_HALO:CONV_HALO + tm, :] = xm_ref[...]
    xw_ref[CONV_HALO + tm:, :] = xn_ref[...]
    h = _modulated(xw_ref[...], g_ref[...], sc_ref[row:row + 1, :], sh_ref[row:row + 1, :])
    y = jnp.dot(h.astype(bf16), w1_ref[...], preferred_element_type=f32) + b1_ref[...]
    win_ref[...] = y[:, :D_MODEL] * jax.nn.sigmoid(y[:, D_MODEL:])

    @pl.when(i == 0)
    def _():
        win_ref[0:CONV_HALO, :] = jnp.zeros((CONV_HALO, D_MODEL), f32)

    @pl.when(i == n_tiles - 1)
    def _():
        win_ref[CONV_HALO + tm:, :] = jnp.zeros((CONV_HALO, D_MODEL), f32)

    off = CONV_HALO - CONV_WIDTH // 2

    def body(r, carry):
        base = pl.multiple_of(r * rb, rb)
        for lb in range(D_MODEL // LANES):
            ls = slice(lb * LANES, (lb + 1) * LANES)
            win = win_ref[pl.ds(base, rb + 2 * CONV_HALO), ls]
            out = jnp.broadcast_to(dwb_ref[:, ls], (rb, LANES))
            for s in range(8):
                z = None
                for k in range(CONV_WIDTH):
                    if (off + k) % 8 != s:
                        continue
                    j = (off + k) // 8
                    term = win[8 * j:8 * j + rb + 8, :] * dw_ref[k:k + 1, ls]
                    z = term if z is None else z + term
                if z is None:
                    continue
                out = out + (z[0:rb] if s == 0 else pltpu.roll(z, rb + 8 - s, axis=0)[0:rb])
            acc_ref[:, ls] = out
        acc = acc_ref[...]
        mu = jnp.mean(acc, axis=-1, keepdims=True)
        xc = acc - mu
        var = jnp.mean(xc * xc, axis=-1, keepdims=True)
        yn = xc * lax.rsqrt(var + NORM_EPS) * ng_ref[...]
        y_ref[pl.ds(base, rb), :] = _silu(yn).astype(bf16)
        return carry

    lax.fori_loop(0, tm // rb, body, 0)
    o = jnp.dot(y_ref[...], w2_ref[...], preferred_element_type=f32)
    o_ref[...] = xm_ref[...] + gate_ref[row:row + 1, :] * o


def _conv_mixer(x, mods, layer, row, gain, w1, b1, dw, dw_b, norm_g, w2, tm, ffn_w=None):
    t = x.shape[0]
    cast = ffn_w is not None
    c_in, c_out, c_shape = _cast_io(layer) if cast else ([], [], [])
    n_tiles = t // tm
    hb = tm // CONV_HALO
    n_hb = t // CONV_HALO
    return pl.pallas_call(
        functools.partial(_conv_kernel, row, n_tiles, cast),
        out_shape=[jax.ShapeDtypeStruct((t, D_MODEL), f32)] + c_shape,
        grid=(n_tiles,),
        in_specs=[
            pl.BlockSpec((tm, D_MODEL), lambda i: (i, 0)),
            pl.BlockSpec((CONV_HALO, D_MODEL), lambda i: (jnp.maximum(i * hb - 1, 0), 0)),
            pl.BlockSpec((CONV_HALO, D_MODEL), lambda i: (jnp.minimum((i + 1) * hb, n_hb - 1), 0)),
            _resident((1, D_MODEL)),
            _mod_spec(layer, SHIFT_M),
            _mod_spec(layer, SCALE_M),
            _resident((D_MODEL, 2 * D_MODEL)),
            _resident((1, 2 * D_MODEL)),
            _resident((CONV_WIDTH, D_MODEL)),
            _resident((1, D_MODEL)),
            _resident((1, D_MODEL)),
            _resident((D_MODEL, D_MODEL)),
            _mod_spec(layer, GATE_M),
        ] + c_in,
        out_specs=[pl.BlockSpec((tm, D_MODEL), lambda i: (i, 0))] + c_out,
        scratch_shapes=[
            pltpu.VMEM((tm + 2 * CONV_HALO, D_MODEL), f32),
            pltpu.VMEM((tm + 2 * CONV_HALO, D_MODEL), f32),
            pltpu.VMEM((CONV_ROWS, D_MODEL), f32),
            pltpu.VMEM((tm, D_MODEL), bf16),
        ],
        compiler_params=_params("arbitrary"),
        name="conv_mixer",
    )(x, x, x, gain, mods, mods, w1, b1, dw, dw_b, norm_g, w2, mods, *(ffn_w or ()))


def _log_sigmoid(v):
    return jnp.minimum(v, 0.0) - jnp.log1p(jnp.exp(-jnp.abs(v)))


def _ret_in_kernel(row, x_ref, g_ref, sh_ref, sc_ref, w_ref, rcos_ref, ccos_ref, rsin_ref, csin_ref,
                   decb_ref, s0b_ref, q_ref, k_ref, v_ref, sg_ref, sball_ref, sb_ref):
    h = _modulated(x_ref[...], g_ref[...], sc_ref[row:row + 1, :], sh_ref[row:row + 1, :]).astype(bf16)
    hk = RET_HEADS * RET_QK_DIM
    hv = RET_HEADS * RET_V_DIM
    cos_t = _tile_table(rcos_ref, ccos_ref)
    sin_t = _tile_table(rsin_ref, csin_ref)

    def rope(y, scale):
        outs = []
        for b in range(hk // LANES):
            yb = y[:, b * LANES:(b + 1) * LANES]
            tb = (b % 2) * LANES
            rot = yb * cos_t[:, tb:tb + LANES] + pltpu.roll(yb, LANES // 2, axis=1) * sin_t[:, tb:tb + LANES]
            outs.append(rot * scale)
        return jnp.concatenate(outs, axis=1)

    q = jnp.dot(h, w_ref[:, 0:hk], preferred_element_type=f32)
    q_ref[...] = rope(q, 1.0).astype(bf16)
    k = rope(jnp.dot(h, w_ref[:, hk:2 * hk], preferred_element_type=f32), RET_QK_DIM ** -0.5).astype(bf16)
    k_ref[...] = k
    v = jnp.dot(h, w_ref[:, 2 * hk:2 * hk + hv], preferred_element_type=f32).astype(bf16)
    v_ref[...] = v
    g = jnp.dot(h, w_ref[:, 2 * hk + hv:], preferred_element_type=f32)
    sg_ref[...] = _silu(g).astype(bf16)

    c = RET_CHUNK

    @pl.when(pl.program_id(0) == 0)
    def _():
        sb_ref[...] = s0b_ref[...]

    idx = lax.broadcasted_iota(jnp.int32, (c, 1), 0).astype(f32)
    for ci in reversed(range(x_ref.shape[0] // c)):
        rows = slice(ci * c, (ci + 1) * c)
        for hh in range(RET_HEADS):
            lg_b = _log_sigmoid(decb_ref[hh, 0:1, :])
            kd = jnp.exp(lg_b[:, 0:1] * idx)
            kh = (k[rows, hh * RET_QK_DIM:(hh + 1) * RET_QK_DIM].astype(f32) * kd).astype(bf16)
            a = lax.dot_general(kh, v[rows, hh * RET_V_DIM:(hh + 1) * RET_V_DIM], (((0,), (0,)), ((), ())),
                                preferred_element_type=f32)
            s = sb_ref[hh]
            sball_ref[ci, hh] = s.astype(bf16)
            sb_ref[hh] = s * jnp.exp(lg_b * c) + a


def _ret_in(x, mods, layer, row, gain, w_in, tabs, dec_b, s0b, tm):
    t = x.shape[0]
    n = t // tm
    hk = RET_HEADS * RET_QK_DIM
    hv = RET_HEADS * RET_V_DIM
    st = (RET_HEADS, RET_QK_DIM, RET_V_DIM)
    tok = lambda w: pl.BlockSpec((tm, w), lambda i: (n - 1 - i, 0))
    row_tab = pl.BlockSpec((tm // GRID_W, RET_QK_DIM), lambda i: (n - 1 - i, 0))
    col_tab = _resident((GRID_W, RET_QK_DIM))
    return pl.pallas_call(
        functools.partial(_ret_in_kernel, row),
        out_shape=[jax.ShapeDtypeStruct((t, hk), bf16), jax.ShapeDtypeStruct((t, hk), bf16),
                   jax.ShapeDtypeStruct((t, hv), bf16), jax.ShapeDtypeStruct((t, hv), bf16),
                   jax.ShapeDtypeStruct((t // RET_CHUNK,) + st, bf16), jax.ShapeDtypeStruct(st, f32)],
        grid=(n,),
        in_specs=[
            tok(D_MODEL),
            _resident((1, D_MODEL)),
            _mod_spec(layer, SHIFT_M),
            _mod_spec(layer, SCALE_M),
            _resident((D_MODEL, 2 * hk + 2 * hv)),
            row_tab, col_tab, row_tab, col_tab,
            _resident((RET_HEADS, 8, RET_V_DIM)),
            _resident(st),
        ],
        out_specs=[tok(hk), tok(hk), tok(hv), tok(hv),
                   pl.BlockSpec((tm // RET_CHUNK,) + st, lambda i: (n - 1 - i, 0, 0, 0)),
                   pl.BlockSpec(st, lambda i: (0, 0, 0))],
        compiler_params=_params("arbitrary"),
        name="ret_in",
    )(x, gain, mods, mods, w_in, *tabs, dec_b, s0b)


def _ret_out_kernel(row, cast, *refs):
    ins, (o_ref, sf_ref), (dec_ref, y_ref), cast_refs = _split_refs(refs, 11, 2, cast)
    q_ref, k_ref, v_ref, sg_ref, sb_ref, s0f_ref, decf_ref, decb_ref, w_ref, x_ref, gate_ref = ins
    _cast_weights(cast_refs)
    c = RET_CHUNK

    @pl.when(pl.program_id(0) == 0)
    def _():
        sf_ref[...] = s0f_ref[...]
        t_i = lax.broadcasted_iota(jnp.int32, (c, c), 0)
        m_i = lax.broadcasted_iota(jnp.int32, (c, c), 1)
        rel = (t_i - m_i).astype(f32)
        for h in range(RET_HEADS):
            lg_f = _log_sigmoid(decf_ref[h, 0:1, :c])
            lg_b = _log_sigmoid(decb_ref[h, 0:1, :c])
            d_f = jnp.where(rel >= 0, jnp.exp(lg_f * jnp.maximum(rel, 0.0)), 0.0)
            d_b = jnp.where(rel <= 0, jnp.exp(lg_b * jnp.maximum(-rel, 0.0)), 0.0)
            dec_ref[h] = d_f + d_b

    idx = lax.broadcasted_iota(jnp.int32, (c, 1), 0).astype(f32)
    for h in range(RET_HEADS):
        lg_f = _log_sigmoid(decf_ref[h, 0:1, :])
        lg_b = _log_sigmoid(decb_ref[h, 0:1, :])
        q = q_ref[:, h * RET_QK_DIM:(h + 1) * RET_QK_DIM]
        k = k_ref[:, h * RET_QK_DIM:(h + 1) * RET_QK_DIM]
        v = v_ref[:, h * RET_V_DIM:(h + 1) * RET_V_DIM]
        s_f = sf_ref[h]
        s = lax.dot_general(q, k, (((1,), (1,)), ((), ())), preferred_element_type=f32)
        o = jnp.dot((s * dec_ref[h]).astype(bf16), v, preferred_element_type=f32)
        o = o + jnp.exp(lg_f[:, 0:1] * (idx + 1.0)) * jnp.dot(q, s_f.astype(bf16), preferred_element_type=f32)
        o = o + jnp.exp(lg_b[:, 0:1] * (c - idx)) * jnp.dot(q, sb_ref[h], preferred_element_type=f32)
        o = o * lax.rsqrt(jnp.mean(o * o, axis=-1, keepdims=True) + NORM_EPS)
        sg = sg_ref[:, h * RET_V_DIM:(h + 1) * RET_V_DIM].astype(f32)
        y_ref[:, h * RET_V_DIM:(h + 1) * RET_V_DIM] = (sg * o).astype(bf16)
        kd = (k.astype(f32) * jnp.exp(lg_f[:, 0:1] * (c - 1.0 - idx))).astype(bf16)
        a = lax.dot_general(kd, v, (((0,), (0,)), ((), ())), preferred_element_type=f32)
        sf_ref[h] = s_f * jnp.exp(lg_f * c) + a
    out = jnp.dot(y_ref[...], w_ref[...], preferred_element_type=f32)
    o_ref[...] = x_ref[...] + gate_ref[row:row + 1, :] * out


def _ret_out(q, k, v, sg, sb_all, s0f, dec_f, dec_b, w_out, x, mods, layer, row, ffn_w=None):
    t = x.shape[0]
    cast = ffn_w is not None
    c_in, c_out, c_shape = _cast_io(layer) if cast else ([], [], [])
    n = t // RET_CHUNK
    hk = RET_HEADS * RET_QK_DIM
    hv = RET_HEADS * RET_V_DIM
    st = (RET_HEADS, RET_QK_DIM, RET_V_DIM)
    tok = lambda w: pl.BlockSpec((RET_CHUNK, w), lambda i: (i, 0))
    return pl.pallas_call(
        functools.partial(_ret_out_kernel, row, cast),
        out_shape=[jax.ShapeDtypeStruct((t, D_MODEL), f32), jax.ShapeDtypeStruct(st, f32)] + c_shape,
        grid=(n,),
        in_specs=[tok(hk), tok(hk), tok(hv), tok(hv),
                  pl.BlockSpec((None,) + st, lambda i: (i, 0, 0, 0)),
                  _resident(st),
                  _resident((RET_HEADS, 8, RET_V_DIM)), _resident((RET_HEADS, 8, RET_V_DIM)),
                  _resident((hv, D_MODEL)),
                  tok(D_MODEL),
                  _mod_spec(layer, GATE_M)] + c_in,
        out_specs=[tok(D_MODEL), pl.BlockSpec(st, lambda i: (0, 0, 0))] + c_out,
        scratch_shapes=[pltpu.VMEM((RET_HEADS, RET_CHUNK, RET_CHUNK), f32),
                        pltpu.VMEM((RET_CHUNK, hv), bf16)],
        compiler_params=_params("arbitrary"),
        name="ret_out",
    )(q, k, v, sg, sb_all, s0f, dec_f, dec_b, w_out, x, mods, *(ffn_w or ()))


def _tile_table_t(row_ref, col_ref):
    rt, ct = row_ref[...], col_ref[...]
    return jnp.concatenate([rt[:, r:r + 1] + ct for r in range(rt.shape[1])], axis=1)


def _att_in_kernel(row, x_ref, g_ref, sh_ref, sc_ref, w_ref, qg_ref, qgp_ref, kg_ref, kgp_ref,
                   rcos_ref, ccos_ref, rsin_ref, csin_ref, qt_ref, k_ref, vt_ref):
    h = _modulated(x_ref[...], g_ref[...], sc_ref[row:row + 1, :], sh_ref[row:row + 1, :]).astype(bf16)
    hd = ATT_HEAD_DIM
    q4 = hd // 4
    nqd = ATT_Q_HEADS * hd
    nkd = ATT_KV_HEADS * hd

    yt = lax.dot_general(w_ref[...], h, (((1,), (1,)), ((), ())), preferred_element_type=f32)
    vt_ref[...] = yt[nqd + nkd:, :].astype(bf16)

    def partner_rows(t):
        return jnp.concatenate([t[q4:2 * q4], t[0:q4], t[3 * q4:], t[2 * q4:3 * q4]], axis=0)

    cos_t = _tile_table_t(rcos_ref, ccos_ref)
    sin_t = _tile_table_t(rsin_ref, csin_ref)

    def norm_rope(y, cos_g, sin_g):
        ms = jnp.sum(y * y, axis=0, keepdims=True) * (1.0 / hd)
        return (y * cos_g + partner_rows(y) * sin_g) * lax.rsqrt(ms + NORM_EPS)

    scale = hd ** -0.5 * LOG2_E
    cos_q, sin_q = cos_t * (qg_ref[...] * scale), sin_t * (qgp_ref[...] * scale)
    for hh in range(ATT_Q_HEADS):
        qt_ref[hh * hd:(hh + 1) * hd, :] = norm_rope(yt[hh * hd:(hh + 1) * hd, :], cos_q, sin_q).astype(bf16)
    cos_k, sin_k = cos_t * kg_ref[...], sin_t * kgp_ref[...]
    k_t = jnp.concatenate([norm_rope(yt[nqd + kh * hd:nqd + (kh + 1) * hd, :], cos_k, sin_k)
                           for kh in range(ATT_KV_HEADS)], axis=0)
    k_ref[...] = k_t.T.astype(bf16)


def _att_in(x, mods, layer, row, gain, w_t, gains, tabs_t, tm):
    t = x.shape[0]
    nqd = ATT_Q_HEADS * ATT_HEAD_DIM
    nkd = ATT_KV_HEADS * ATT_HEAD_DIM
    gr = tm // GRID_W
    row_t = pl.BlockSpec((None, ATT_HEAD_DIM, gr), lambda i: (i, 0, 0))
    col_t = _resident((ATT_HEAD_DIM, GRID_W))
    gain_col = _resident((ATT_HEAD_DIM, 1))
    return pl.pallas_call(
        functools.partial(_att_in_kernel, row),
        out_shape=[jax.ShapeDtypeStruct((nqd, t), bf16), jax.ShapeDtypeStruct((t, nkd), bf16),
                   jax.ShapeDtypeStruct((nkd, t), bf16)],
        grid=(t // tm,),
        in_specs=[
            pl.BlockSpec((tm, D_MODEL), lambda i: (i, 0)),
            _resident((1, D_MODEL)),
            _mod_spec(layer, SHIFT_M),
            _mod_spec(layer, SCALE_M),
            _resident((nqd + 2 * nkd, D_MODEL)),
            gain_col, gain_col, gain_col, gain_col,
            row_t, col_t, row_t, col_t,
        ],
        out_specs=[pl.BlockSpec((nqd, tm), lambda i: (0, i)), pl.BlockSpec((tm, nkd), lambda i: (i, 0)),
                   pl.BlockSpec((nkd, tm), lambda i: (0, i))],
        compiler_params=_params("arbitrary"),
        name="att_in",
    )(x, gain, mods, mods, w_t, *gains, *tabs_t)


def _att_kernel(row, n_blocks, band, cast, *refs):
    ins, (o_ref,), scratch, cast_refs = _split_refs(refs, 13 if band else 7, 1, cast)
    s_refs, e_refs, yt_ref = scratch[:ATT_KV_HEADS], scratch[ATT_KV_HEADS:2 * ATT_KV_HEADS], scratch[-1]
    if band:
        sink_ref, qt_ref, kc_ref, vtc_ref, kp_ref, kn_ref, kx_ref, vtp_ref, vtn_ref, vtx_ref, w_ref, x_ref, gate_ref = ins
    else:
        sink_ref, qt_ref, kc_ref, vtc_ref, w_ref, x_ref, gate_ref = ins
    _cast_weights(cast_refs)
    i = pl.program_id(0)
    c = ATT_BLOCK
    g = ATT_GROUP
    hd = ATT_HEAD_DIM
    n_ctx = kc_ref.shape[0]
    if band:
        key = lax.broadcasted_iota(jnp.int32, (c, g * c), 0)
        qry = lax.broadcasted_iota(jnp.int32, (c, g * c), 1) % c
        prev_ok = (key >= qry + (c - ATT_WINDOW)) & (i > 0)
        next_ok = (key <= qry + (ATT_WINDOW - c)) & (i < n_blocks - 1)
    vals_t = []
    for kh in range(ATT_KV_HEADS):
        sl = slice((kh // 2) * LANES, (kh // 2 + 1) * LANES)
        vs = slice(kh * hd, (kh + 1) * hd)
        q_t = jnp.concatenate([qt_ref[(kh * g + gg) * hd:(kh * g + gg + 1) * hd, :] for gg in range(g)], axis=1)
        pad = jnp.zeros_like(q_t)
        q_t = jnp.concatenate([q_t, pad] if kh % 2 == 0 else [pad, q_t], axis=0)
        if band:
            keys = jnp.concatenate([kc_ref[:, sl], kp_ref[:, sl], kx_ref[:, sl], kn_ref[:, sl]], axis=0)
            vals_t.append(jnp.concatenate([vtc_ref[vs, :], vtp_ref[vs, :], vtx_ref[vs, :], vtn_ref[vs, :]], axis=1))
        else:
            keys = kc_ref[:, sl]
            vals_t.append(vtc_ref[vs, :])
        s_refs[kh][...] = jnp.dot(keys, q_t, preferred_element_type=f32)
    for kh in range(ATT_KV_HEADS):
        s = s_refs[kh][...]
        if band:
            s = jnp.concatenate([
                s[:n_ctx],
                jnp.where(prev_ok, s[n_ctx:n_ctx + c], NEG_INF),
                s[n_ctx + c:n_ctx + 2 * c],
                jnp.where(next_ok, s[n_ctx + 2 * c:], NEG_INF)], axis=0)
        sink = jnp.concatenate([jnp.full((1, c), sink_ref[kh * g + gg] * LOG2_E, f32) for gg in range(g)], axis=1)
        m = jnp.maximum(jnp.max(s, axis=0, keepdims=True), sink)
        e = jnp.exp2(s - m)
        e_refs[kh][...] = e.astype(bf16)
        ones = jnp.ones((16, vals_t[kh].shape[1]), bf16)
        o_t = jnp.dot(jnp.concatenate([vals_t[kh], ones], axis=0), e_refs[kh][...], preferred_element_type=f32)
        denom = o_t[hd:hd + 1, :] + jnp.exp2(sink - m)
        o_t = o_t[:hd, :] * (1.0 / denom)
        for gg in range(g):
            h = kh * g + gg
            yt_ref[h * hd:(h + 1) * hd, :] = o_t[:, gg * c:(gg + 1) * c].astype(bf16)
    out = lax.dot_general(yt_ref[...], w_ref[...], (((0,), (0,)), ((), ())), preferred_element_type=f32)
    o_ref[...] = x_ref[...] + gate_ref[row:row + 1, :] * out


def _attention(sink, q_t, kc, vtc, kx, vtx, w_o, x, mods, layer, row, ffn_w=None):
    t = x.shape[0]
    cast = ffn_w is not None
    c_in, c_out, c_shape = _cast_io(layer) if cast else ([], [], [])
    c = ATT_BLOCK
    n = t // c
    nqd = ATT_Q_HEADS * ATT_HEAD_DIM
    nk = nv = ATT_KV_HEADS * ATT_HEAD_DIM
    band = kx is not None
    n_keys = kc.shape[0] + (3 * c if band else 0)
    tok = lambda w: pl.BlockSpec((c, w), lambda i: (i, 0))
    before = lambda i: jnp.maximum(i - 1, 0)
    after = lambda i: jnp.minimum(i + 1, n - 1)
    in_specs = [pl.BlockSpec(memory_space=pltpu.SMEM), pl.BlockSpec((nqd, c), lambda i: (0, i)),
                _resident(kc.shape), _resident(vtc.shape)]
    args = [sink, q_t, kc, vtc]
    if band:
        in_specs += [pl.BlockSpec((c, nk), lambda i: (before(i), 0)), pl.BlockSpec((c, nk), lambda i: (after(i), 0)),
                     tok(nk),
                     pl.BlockSpec((nv, c), lambda i: (0, before(i))), pl.BlockSpec((nv, c), lambda i: (0, after(i))),
                     pl.BlockSpec((nv, c), lambda i: (0, i))]
        args += [kx, kx, kx, vtx, vtx, vtx]
    in_specs += [_resident((nqd, D_MODEL)), tok(D_MODEL), _mod_spec(layer, GATE_M)] + c_in
    args += [w_o, x, mods, *(ffn_w or ())]
    return pl.pallas_call(
        functools.partial(_att_kernel, row, n, band, cast),
        out_shape=[jax.ShapeDtypeStruct((t, D_MODEL), f32)] + c_shape,
        grid=(n,),
        in_specs=in_specs,
        out_specs=[tok(D_MODEL)] + c_out,
        scratch_shapes=([pltpu.VMEM((n_keys, ATT_GROUP * c), f32)] * ATT_KV_HEADS
                        + [pltpu.VMEM((n_keys, ATT_GROUP * c), bf16)] * ATT_KV_HEADS
                        + [pltpu.VMEM((nqd, c), bf16)]),
        compiler_params=_params("arbitrary"),
        name="att_band" if band else "att_ctx",
    )(*args)


def _ffn_kernel(row, x_ref, g_ref, sh_ref, sc_ref, gate_ref, wgu_ref, wd_ref, o_ref):
    x = x_ref[...]
    h = _modulated(x, g_ref[...], sc_ref[row:row + 1, :], sh_ref[row:row + 1, :]).astype(bf16)
    acc = jnp.zeros(x.shape, f32)
    for c0 in range(0, FFN_HIDDEN, FFN_CHUNK):
        a = jnp.dot(h, wgu_ref[:, c0:c0 + FFN_CHUNK], preferred_element_type=f32)
        b = jnp.dot(h, wgu_ref[:, FFN_HIDDEN + c0:FFN_HIDDEN + c0 + FFN_CHUNK], preferred_element_type=f32)
        act = (_silu(a) * b).astype(bf16)
        acc = acc + jnp.dot(act, wd_ref[c0:c0 + FFN_CHUNK, :], preferred_element_type=f32)
    o_ref[...] = x + gate_ref[row:row + 1, :] * acc


def _ffn(x, mods, layer, row, gain, w_gu, w_down, tm):
    t = x.shape[0]
    tok = pl.BlockSpec((tm, D_MODEL), lambda i: (i, 0))
    return pl.pallas_call(
        functools.partial(_ffn_kernel, row),
        out_shape=jax.ShapeDtypeStruct((t, D_MODEL), f32),
        grid=(t // tm,),
        in_specs=[tok, _resident((1, D_MODEL)),
                  _mod_spec(layer, SHIFT_F), _mod_spec(layer, SCALE_F), _mod_spec(layer, GATE_F),
                  _resident((D_MODEL, 2 * FFN_HIDDEN)), _resident((FFN_HIDDEN, D_MODEL))],
        out_specs=tok,
        compiler_params=_params("arbitrary"),
        name="ffn",
    )(x, gain, mods, mods, mods, w_gu, w_down)


def _rope_tables(t, quarter, layout):
    n_rows = t // GRID_W
    inv = ROPE_BASE ** (-jnp.arange(quarter, dtype=f32) / quarter)
    ang = {'r': jnp.arange(n_rows).astype(f32)[:, None] * inv, 'c': jnp.arange(GRID_W).astype(f32)[:, None] * inv}
    n = {'r': n_rows, 'c': GRID_W}

    def table(axis, fn, signed):
        parts = []
        for grp in layout:
            if isinstance(grp, int):
                parts.append(jnp.zeros((n[axis], grp), f32))
            elif grp[0] == axis:
                sign = -1.0 if (signed and grp[1] == '1') else 1.0
                parts.append(sign * fn(ang[axis]))
            else:
                parts.append(jnp.zeros((n[axis], quarter), f32))
        return jnp.concatenate(parts, axis=1)

    return (table('r', jnp.cos, False), table('c', jnp.cos, False),
            table('r', jnp.sin, True), table('c', jnp.sin, True))


RET_ROPE_LAYOUT = ('r1', 'r2', 'c1', 'c2')
ATT_ROPE_LAYOUT = ('r1', 'r2', 'c1', 'c2')


def _transposed_tables(tabs, tm):
    gr = tm // GRID_W
    row_t = lambda a: a.T.reshape(a.shape[1], a.shape[0] // gr, gr).transpose(1, 0, 2)
    return row_t(tabs[0]), tabs[1].T, row_t(tabs[2]), tabs[3].T


def _identity_tables(t, width):
    n_rows = t // GRID_W
    return (jnp.ones((n_rows, width), f32), jnp.zeros((GRID_W, width), f32),
            jnp.zeros((n_rows, width), f32), jnp.zeros((GRID_W, width), f32))


def kernel(x, c, ctx, c_ctx, ada_w, ada_b, norm_mix, norm_ffn, conv_w1, conv_b1, conv_dw, conv_dw_b, conv_norm, conv_w2, ret_w_in, ret_decay_f, ret_decay_b, ret_w_out, att_w_qkv, att_q_norm, att_k_norm, att_sink, att_w_o, ffn_w_gu, ffn_w_down):
    assert x.shape[0] == 1 and c.shape[0] == 1 and ctx.shape[0] == 1
    t_lat, t_ctx = x.shape[1], ctx.shape[1]
    tm_lat, tm_ctx = 512, t_ctx
    tm_ffn = 1024
    xs, hc = x[0], ctx[0]
    lat, cx = 0, 1

    cond_t = jnp.zeros((D_MODEL, 8), f32).at[:, lat].set(c[0]).at[:, cx].set(c_ctx)
    mods = _adaln(cond_t, ada_w, ada_b)

    row1 = lambda v: v.reshape(1, -1)
    ffn_w = (ffn_w_gu, ffn_w_down)
    for i in range(DEPTH):
        kind, j, last = i % N_MIXERS, i // N_MIXERS, i == DEPTH - 1
        with_ctx = not last
        g_mix = row1(norm_mix[i])
        if kind == 0:
            w1, w2 = conv_w1[j].astype(bf16), conv_w2[j].astype(bf16)
            cargs = (g_mix, w1, row1(conv_b1[j]), conv_dw[j], row1(conv_dw_b[j]), row1(conv_norm[j]), w2)
            xs, w_gu, w_down = _conv_mixer(xs, mods, i, lat, *cargs, tm_lat, ffn_w=ffn_w)
            if with_ctx:
                hc, = _conv_mixer(hc, mods, i, cx, *cargs, tm_ctx)
        elif kind == 1:
            w_in, w_out = ret_w_in[j].astype(bf16), ret_w_out[j].astype(bf16)
            bdec = lambda d: jnp.broadcast_to(d[:, None, None], (RET_HEADS, 8, RET_V_DIM)).astype(f32)
            dec_f, dec_b = bdec(ret_decay_f[j]), bdec(ret_decay_b[j])
            zeros = jnp.zeros((RET_HEADS, RET_QK_DIM, RET_V_DIM), f32)
            qc, kc, vc, gc, sb_c, s0b = _ret_in(hc, mods, i, cx, g_mix, w_in, _identity_tables(t_ctx, RET_QK_DIM),
                                                dec_b, zeros, tm_ctx)
            hc_new, s0f = _ret_out(qc, kc, vc, gc, sb_c, zeros, dec_f, dec_b, w_out, hc, mods, i, cx)
            qx, kx, vx, gx, sb_x, _ = _ret_in(xs, mods, i, lat, g_mix, w_in,
                                              _rope_tables(t_lat, RET_QK_DIM // 4, RET_ROPE_LAYOUT), dec_b, s0b, tm_lat)
            xs, _, w_gu, w_down = _ret_out(qx, kx, vx, gx, sb_x, s0f, dec_f, dec_b, w_out, xs, mods, i, lat, ffn_w=ffn_w)
            if with_ctx:
                hc = hc_new
        else:
            w_t = att_w_qkv[j].T.astype(bf16)
            w_o = att_w_o[j].astype(bf16)
            q4 = ATT_HEAD_DIM // 4
            partner = lambda v: jnp.concatenate([v[q4:2 * q4], v[:q4], v[3 * q4:], v[2 * q4:3 * q4]])
            col = lambda v: v.astype(f32).reshape(-1, 1)
            gains = (col(att_q_norm[j]), col(partner(att_q_norm[j])), col(att_k_norm[j]), col(partner(att_k_norm[j])))
            sink = att_sink[j].astype(f32)
            tabs_c = _transposed_tables(_identity_tables(t_ctx, ATT_HEAD_DIM), tm_ctx)
            tabs_x = _transposed_tables(_rope_tables(t_lat, q4, ATT_ROPE_LAYOUT), tm_lat)
            qc, kc, vtc = _att_in(hc, mods, i, cx, g_mix, w_t, gains, tabs_c, tm_ctx)
            qx, kx, vtx = _att_in(xs, mods, i, lat, g_mix, w_t, gains, tabs_x, tm_lat)
            xs, w_gu, w_down = _attention(sink, qx, kc, vtc, kx, vtx, w_o, xs, mods, i, lat, ffn_w=ffn_w)
            if with_ctx:
                hc, = _attention(sink, qc, kc, vtc, None, None, w_o, hc, mods, i, cx)
        g_ffn = row1(norm_ffn[i])
        xs = _ffn(xs, mods, i, lat, g_ffn, w_gu, w_down, tm_ffn)
        if with_ctx:
            hc = _ffn(hc, mods, i, cx, g_ffn, w_gu, w_down, tm_ctx)
    return xs[None]
```
